```python
import math
import jax, jax.numpy as jnp
from jax import lax
import numpy as np

D_MODEL = 1024
BATCH = 2
SEQ = 8192
DEPTH = 4

CHUNK = 64
N_A = DEPTH // 2
N_B = DEPTH - N_A
HGRN_EXPAND = 128
HGRN_HEADS = D_MODEL // 128
HGRN_FDIM = HGRN_HEADS * HGRN_EXPAND
HGRN_VDIM = D_MODEL // HGRN_HEADS
DIFF_HEADS = D_MODEL // 128
DIFF_HEAD_DIM = D_MODEL // DIFF_HEADS // 2
Q_BLOCK = 128
REL_BUCKETS = 32
REL_MAX_DIST = 128
D_FF = 4 * D_MODEL
EPS = 1e-6
NEG_INF = -1e30

kernel_name = "yoco_hgrn2_diffattn_trunk"


def rms_norm(x, w):
    xf = x.astype(jnp.float32)
    y = xf * lax.rsqrt(jnp.mean(xf * xf, axis=-1, keepdims=True) + EPS)
    return (y * w.astype(jnp.float32)).astype(x.dtype)


def squared_relu_mlp(h, w_up, w_down):
    u = jax.nn.relu(h @ w_up)
    return (u * u) @ w_down


def hgrn2_recurrence(q, k, log_f, v):
    B, S, H, dk = q.shape
    dv = v.shape[-1]
    n = S // CHUNK

    def to_chunks(t):
        return t.astype(jnp.float32).reshape(B, n, CHUNK, H, t.shape[-1]).transpose(1, 0, 3, 2, 4)

    causal = jnp.tril(jnp.ones((CHUNK, CHUNK), dtype=bool))[None, None, :, :, None]

    def step(state, inp):
        qc, kc, gc, vc = inp
        b = jnp.cumsum(gc, axis=2)
        o_inter = jnp.einsum('bhtk,bhkv->bhtv', qc * jnp.exp(b), state)
        diff = b[:, :, :, None, :] - b[:, :, None, :, :]
        decay = jnp.where(causal, jnp.exp(jnp.minimum(diff, 0.0)), 0.0)
        scores = jnp.einsum('bhtk,bhtsk,bhsk->bhts', qc, decay, kc)
        o = o_inter + jnp.einsum('bhts,bhsv->bhtv', scores, vc)
        b_last = b[:, :, -1:, :]
        state = jnp.exp(b_last[:, :, 0, :])[..., None] * state + jnp.einsum(
            'bhsk,bhsv->bhkv', kc * jnp.exp(b_last - b), vc)
        return state, o

    s0 = jnp.zeros((B, H, dk, dv), jnp.float32)
    _, o = lax.scan(step, s0, (to_chunks(q), to_chunks(k), to_chunks(log_f), to_chunks(v)))
    return o.transpose(1, 0, 3, 2, 4).reshape(B, S, H, dv)


def hgrn2_mixer(h, w_in, lb, gate_norm_w, w_out):
    B, S, _ = h.shape
    proj = h @ w_in
    q, f, i, g = jnp.split(proj, [HGRN_FDIM, 2 * HGRN_FDIM, 2 * HGRN_FDIM + D_MODEL], axis=-1)
    q = jax.nn.silu(q)
    f32 = f.astype(jnp.float32)
    log_f = jnp.logaddexp(jnp.log(lb), jnp.log1p(-lb) + jax.nn.log_sigmoid(f32))
    k = (1.0 - lb) * jax.nn.sigmoid(-f32)
    shp = (B, S, HGRN_HEADS, HGRN_EXPAND)
    o = hgrn2_recurrence(q.reshape(shp), k.reshape(shp), log_f.reshape(shp),
                         i.reshape(B, S, HGRN_HEADS, HGRN_VDIM))
    o = rms_norm(o, gate_norm_w) * jax.nn.silu(g.reshape(B, S, HGRN_HEADS, HGRN_VDIM).astype(jnp.float32))
    return o.reshape(B, S, D_MODEL).astype(h.dtype) @ w_out


def rel_bucket(rel):
    half = REL_BUCKETS // 2
    max_exact = half // 2
    ret = jnp.where(rel > 0, half, 0)
    n = jnp.abs(rel)
    nf = jnp.maximum(n, 1).astype(jnp.float32)
    large = max_exact + (jnp.log(nf / max_exact) / math.log(REL_MAX_DIST / max_exact)
                         * (half - max_exact)).astype(jnp.int32)
    large = jnp.minimum(large, half - 1)
    return ret + jnp.where(n < max_exact, n, large)


def diff_attention(q, k, v, rel_bias, lam):
    B, S, H, _, d = q.shape
    nblk = S // Q_BLOCK
    scale = d ** -0.5
    k_pos = jnp.arange(S)
    kf = k.astype(jnp.float32)
    vf = v.astype(jnp.float32)
    qb = q.reshape(B, nblk, Q_BLOCK, H, 2, d).transpose(1, 0, 2, 3, 4, 5)

    def block(args):
        idx, qblk = args
        q_pos = idx * Q_BLOCK + jnp.arange(Q_BLOCK)
        allowed = (k_pos[None, :] // CHUNK) <= (q_pos[:, None] // CHUNK)
        bias = rel_bias[rel_bucket(k_pos[None, :] - q_pos[:, None])]
        bias = bias.reshape(Q_BLOCK, S, H, 2).transpose(2, 3, 0, 1).astype(jnp.float32)
        logits = jnp.einsum('bqhmd,bshmd->bhmqs', qblk.astype(jnp.float32), kf) * scale + bias
        logits = jnp.where(allowed, logits, NEG_INF)
        p = jax.nn.softmax(logits, axis=-1)
        attn = p[:, :, 0] - lam * p[:, :, 1]
        return jnp.einsum('bhqs,bshe->bqhe', attn, vf)

    o = lax.map(block, (jnp.arange(nblk), qb))
    return o.transpose(1, 0, 2, 3, 4).reshape(B, S, H, 2 * d)


def diff_attn_mixer(h, k, v, w_q, lam_params, subln_w, w_out, rel_bias, lam_init):
    B, S, _ = h.shape
    q = (h @ w_q).reshape(B, S, DIFF_HEADS, 2, DIFF_HEAD_DIM)
    lp = lam_params.astype(jnp.float32)
    lam = jnp.exp(jnp.sum(lp[0] * lp[1])) - jnp.exp(jnp.sum(lp[2] * lp[3])) + lam_init
    o = diff_attention(q, k, v, rel_bias, lam)
    o = rms_norm(o, subln_w) * (1.0 - lam_init)
    return o.reshape(B, S, D_MODEL).astype(h.dtype) @ w_out


def shared_kv(h, kv_norm, w_kv):
    B, S, _ = h.shape
    kv = rms_norm(h, kv_norm) @ w_kv
    k = kv[..., :D_MODEL].reshape(B, S, DIFF_HEADS, 2, DIFF_HEAD_DIM)
    v = kv[..., D_MODEL:].reshape(B, S, DIFF_HEADS, 2 * DIFF_HEAD_DIM)
    return k, v


def setup_inputs(seed: int = 0) -> dict:
    key = jax.random.key(seed)
    ks = jax.random.split(key, 24)
    D = D_MODEL

    def w(k, shape, fan_in):
        return jax.random.normal(k, shape, jnp.float32) * fan_in ** -0.5

    def gain(k, shape):
        return 1.0 + 0.05 * jax.random.normal(k, shape, jnp.float32)

    return {
        "x": jax.random.normal(ks[0], (BATCH, SEQ, D), jnp.float32),
        "a_norm_pre": gain(ks[1], (N_A, D)),
        "a_norm_post": gain(ks[2], (N_A, D)),
        "a_w_in": w(ks[3], (N_A, D, 2 * HGRN_FDIM + 2 * D), D),
        "a_lb": 0.1 * jax.random.normal(ks[4], (N_A, HGRN_FDIM), jnp.float32),
        "a_gate_norm": gain(ks[5], (N_A, HGRN_VDIM)),
        "a_w_out": w(ks[6], (N_A, D, D), D),
        "kv_norm": gain(ks[7], (D,)),
        "w_kv": w(ks[8], (D, 2 * D), D),
        "b_norm_pre": gain(ks[9], (N_B, D)),
        "b_norm_post": gain(ks[10], (N_B, D)),
        "b_w_q": w(ks[11], (N_B, D, D), D),
        "b_lambda": 0.1 * jax.random.normal(ks[12], (N_B, 4, DIFF_HEAD_DIM), jnp.float32),
        "b_subln": gain(ks[13], (N_B, 2 * DIFF_HEAD_DIM)),
        "b_w_out": w(ks[14], (N_B, D, D), D),
        "rel_bias": 0.2 * jax.random.normal(ks[15], (REL_BUCKETS, 2 * DIFF_HEADS), jnp.float32),
        "mlp_norm_pre": gain(ks[16], (DEPTH, D)),
        "mlp_norm_post": gain(ks[17], (DEPTH, D)),
        "mlp_w_up": w(ks[18], (DEPTH, D, D_FF), D),
        "mlp_w_down": w(ks[19], (DEPTH, D_FF, D), D_FF),
    }


def reference(x, a_norm_pre, a_norm_post, a_w_in, a_lb, a_gate_norm, a_w_out, kv_norm, w_kv,
              b_norm_pre, b_norm_post, b_w_q, b_lambda, b_subln, b_w_out, rel_bias,
              mlp_norm_pre, mlp_norm_post, mlp_w_up, mlp_w_down):
    lb_all = jnp.cumsum(jax.nn.softmax(a_lb.astype(jnp.float32), axis=0), axis=0)
    lb_all = lb_all - lb_all[0:1]
    h = x
    k_sh = None
    v_sh = None
    for layer in range(DEPTH):
        if layer < N_A:
            a = layer
            mix = hgrn2_mixer(rms_norm(h, a_norm_pre[a]), a_w_in[a], lb_all[a], a_gate_norm[a], a_w_out[a])
            h = h + rms_norm(mix, a_norm_post[a])
        else:
            bi = layer - N_A
            lam_init = 0.8 - 0.6 * math.exp(-0.3 * layer)
            mix = diff_attn_mixer(rms_norm(h, b_norm_pre[bi]), k_sh, v_sh, b_w_q[bi], b_lambda[bi],
                                  b_subln[bi], b_w_out[bi], rel_bias, lam_init)
            h = h + rms_norm(mix, b_norm_post[bi])
        ff = squared_relu_mlp(rms_norm(h, mlp_norm_pre[layer]), mlp_w_up[layer], mlp_w_down[layer])
        h = h + rms_norm(ff, mlp_norm_post[layer])
        if layer == N_A - 1:
            k_sh, v_sh = shared_kv(h, kv_norm, w_kv)
    return h
```

```python
import functools
import math

import jax
import jax.numpy as jnp
from jax import lax
from jax.experimental import pallas as pl
from jax.experimental.pallas import tpu as pltpu

EPS = 1e-6
NEG_INF = -1e30
CHUNK = 64
HEAD = 128
REL_BUCKETS = 32
REL_MAX_DIST = 128
LOG2E = math.log2(math.e)

ROW_TILE = 512
REC_ROWS = 512
REC_SUB = 16
ATTN_TILE = 256
VMEM_LIMIT = 56 * 1024 * 1024

F32 = jnp.float32
BF16 = jnp.bfloat16


def _rms(x, w):
    return x * lax.rsqrt(jnp.mean(x * x, axis=-1, keepdims=True) + EPS) * w


def _sigmoid(x):
    return 1.0 / (1.0 + jnp.exp(-x))


def _params(*sem):
    return pltpu.CompilerParams(dimension_semantics=sem, vmem_limit_bytes=VMEM_LIMIT)


def _resident(shape):
    return pl.BlockSpec(shape, lambda *_: (0,) * len(shape), pipeline_mode=pl.Buffered(1))


def _hgrn_in_kernel(h_ref, nw_ref, w_ref, loglb_ref, log1mlb_ref, onemlb_ref,
                    q_ref, lf_ref, k_ref, v_ref, sg_ref):
    d = h_ref.shape[1]
    xn = _rms(h_ref[...], nw_ref[...]).astype(BF16)

    def proj(c):
        return jnp.dot(xn, w_ref[:, c * d:(c + 1) * d], preferred_element_type=F32)

    q = proj(0)
    q_ref[...] = (q * _sigmoid(q)).astype(BF16)
    f = proj(1)
    log_sig = jnp.minimum(f, 0.0) - jnp.log1p(jnp.exp(-jnp.abs(f)))
    a = loglb_ref[...]
    c = log1mlb_ref[...] + log_sig
    lf_ref[...] = jnp.maximum(a, c) + jnp.log1p(jnp.exp(-jnp.abs(a - c)))
    k_ref[...] = (onemlb_ref[...] / (1.0 + jnp.exp(f))).astype(BF16)
    v_ref[...] = proj(2).astype(BF16)
    g = proj(3)
    sg_ref[...] = (g * _sigmoid(g)).astype(BF16)


def _hgrn_in(h, nw, w, lb):
    t, d = h.shape
    row = lambda i: (i, 0)
    vec = lambda a: a.reshape(1, d).astype(F32)
    out_bf = jax.ShapeDtypeStruct((t, d), BF16)
    return pl.pallas_call(
        _hgrn_in_kernel,
        grid=(t // ROW_TILE,),
        in_specs=[pl.BlockSpec((ROW_TILE, d), row), _resident((1, d)), _resident(w.shape),
                  _resident((1, d)), _resident((1, d)), _resident((1, d))],
        out_specs=[pl.BlockSpec((ROW_TILE, d), row)] * 5,
        out_shape=[out_bf, jax.ShapeDtypeStruct((t, d), F32), out_bf, out_bf, out_bf],
        compiler_params=_params("parallel"),
        name="hgrn_in",
    )(h, vec(nw), w, vec(jnp.log(lb)), vec(jnp.log1p(-lb)), vec(1.0 - lb))


def _hgrn_rec_kernel(q_ref, lf_ref, k_ref, v_ref, sg_ref, gnw_ref, o_ref, st_ref, *, nheads):
    sub = REC_SUB

    @pl.when(pl.program_id(1) == 0)
    def _():
        st_ref[...] = jnp.zeros_like(st_ref)

    r_i = lax.broadcasted_iota(jnp.int32, (sub, sub), 0)
    c_i = lax.broadcasted_iota(jnp.int32, (sub, sub), 1)
    cum = (c_i <= r_i).astype(F32)
    ones_w = jnp.ones((HEAD, HEAD), BF16)
    rows = lax.broadcasted_iota(jnp.int32, (sub, HEAD), 0)
    gnw = gnw_ref[...]

    def body(blk, carry):
        r0 = pl.multiple_of(blk * sub, sub)
        for h in range(nheads):
            cs = slice(h * HEAD, (h + 1) * HEAD)
            g = lf_ref[pl.ds(r0, sub), cs]
            b = jnp.dot(cum, g, precision=lax.Precision.HIGHEST, preferred_element_type=F32)
            q = q_ref[pl.ds(r0, sub), cs].astype(F32)
            k = k_ref[pl.ds(r0, sub), cs].astype(F32)
            v = v_ref[pl.ds(r0, sub), cs]
            vf = v.astype(F32)
            b_last = b[sub - 1:sub, :]
            st = st_ref[h]
            qe = (q * jnp.exp(b)).astype(BF16)
            o = lax.dot_general(qe, st.astype(BF16), (((1,), (1,)), ((), ())),
                                preferred_element_type=F32)
            kd = (k * jnp.exp(b_last - b)).astype(BF16)
            upd = lax.dot_general(v, kd, (((0,), (0,)), ((), ())), preferred_element_type=F32)
            st_ref[h] = st * jnp.exp(b_last) + upd
            pieces = []
            for s in range(sub):
                decay = jnp.exp(jnp.minimum(b - b[s:s + 1, :], 0.0))
                a = jnp.where(rows >= s, q * k[s:s + 1, :] * decay, 0.0)
                pieces.append(a.astype(BF16))
            scores = jnp.dot(jnp.concatenate(pieces, axis=0), ones_w, preferred_element_type=F32)
            for s in range(sub):
                o = o + scores[s * sub:(s + 1) * sub, :] * vf[s:s + 1, :]
            og = _rms(o, gnw) * sg_ref[pl.ds(r0, sub), cs].astype(F32)
            o_ref[pl.ds(r0, sub), cs] = og.astype(BF16)
        return carry

    lax.fori_loop(0, q_ref.shape[0] // sub, body, 0)


def _hgrn_rec(q, lf, k, v, sg, gnw, batch):
    t, d = q.shape
    nheads = d // HEAD
    nt = t // batch // REC_ROWS
    row = lambda b, i: (b * nt + i, 0)
    blk = pl.BlockSpec((REC_ROWS, d), row)
    return pl.pallas_call(
        functools.partial(_hgrn_rec_kernel, nheads=nheads),
        grid=(batch, nt),
        in_specs=[blk, blk, blk, blk, blk, _resident((1, HEAD))],
        out_specs=blk,
        out_shape=jax.ShapeDtypeStruct((t, d), BF16),
        scratch_shapes=[pltpu.VMEM((nheads, HEAD, HEAD), F32)],
        compiler_params=_params("parallel", "arbitrary"),
        name="hgrn_rec",
    )(q, lf, k, v, sg, gnw.reshape(1, HEAD).astype(F32))


def _out_proj_kernel(x_ref, w_ref, h_ref, nw_ref, o_ref):
    y = jnp.dot(x_ref[...], w_ref[...], preferred_element_type=F32)
    o_ref[...] = h_ref[...] + _rms(y, nw_ref[...])


def _out_proj(x, w, h, nw):
    t, d = h.shape
    row = lambda i: (i, 0)
    return pl.pallas_call(
        _out_proj_kernel,
        grid=(t // ROW_TILE,),
        in_specs=[pl.BlockSpec((ROW_TILE, d), row), _resident(w.shape),
                  pl.BlockSpec((ROW_TILE, d), row), _resident((1, d))],
        out_specs=pl.BlockSpec((ROW_TILE, d), row),
        out_shape=jax.ShapeDtypeStruct((t, d), F32),
        compiler_params=_params("parallel"),
        name="out_proj",
    )(x, w, h, nw.reshape(1, d).astype(F32))


def _mlp_kernel(h_ref, n1_ref, wu_ref, wd_ref, n2_ref, o_ref, u_ref):
    h = h_ref[...]
    d = h.shape[1]
    xn = _rms(h, n1_ref[...]).astype(BF16)
    for c in range(wu_ref.shape[1] // d):
        u = jnp.maximum(jnp.dot(xn, wu_ref[:, c * d:(c + 1) * d], preferred_element_type=F32), 0.0)
        u_ref[:, c * d:(c + 1) * d] = (u * u).astype(BF16)
    y = jnp.dot(u_ref[...], wd_ref[...], preferred_element_type=F32)
    o_ref[...] = h + _rms(y, n2_ref[...])


def _mlp(h, n1, wu, wd, n2):
    t, d = h.shape
    row = lambda i: (i, 0)
    vec = lambda a: a.reshape(1, d).astype(F32)
    return pl.pallas_call(
        _mlp_kernel,
        grid=(t // ROW_TILE,),
        in_specs=[pl.BlockSpec((ROW_TILE, d), row), _resident((1, d)), _resident(wu.shape),
                  _resident(wd.shape), _resident((1, d))],
        out_specs=pl.BlockSpec((ROW_TILE, d), row),
        out_shape=jax.ShapeDtypeStruct((t, d), F32),
        scratch_shapes=[pltpu.VMEM((ROW_TILE, wu.shape[1]), BF16)],
        compiler_params=_params("parallel"),
        name="mlp",
    )(h, vec(n1), wu, wd, vec(n2))


def _norm_mm_kernel(h_ref, nw_ref, w_ref, *o_refs, scale):
    d = h_ref.shape[1]
    xn = _rms(h_ref[...], nw_ref[...]).astype(BF16)
    for c, o_ref in enumerate(o_refs):
        y = jnp.dot(xn, w_ref[:, c * d:(c + 1) * d], preferred_element_type=F32)
        o_ref[...] = (y * scale).astype(BF16)


def _norm_mm(h, nw, w, scale=1.0):
    t, d = h.shape
    n_out = w.shape[1] // d
    row = lambda i: (i, 0)
    return pl.pallas_call(
        functools.partial(_norm_mm_kernel, scale=scale),
        grid=(t // ROW_TILE,),
        in_specs=[pl.BlockSpec((ROW_TILE, d), row), _resident((1, d)), _resident(w.shape)],
        out_specs=[pl.BlockSpec((ROW_TILE, d), row)] * n_out,
        out_shape=[jax.ShapeDtypeStruct((t, d), BF16)] * n_out,
        compiler_params=_params("parallel"),
        name="norm_mm",
    )(h, nw.reshape(1, d).astype(F32), w)


def _rel_bucket(rel):
    half = REL_BUCKETS // 2
    max_exact = half // 2
    ret = jnp.where(rel > 0, half, 0)
    n = jnp.abs(rel)
    nf = jnp.maximum(n, 1).astype(jnp.float32)
    large = max_exact + (jnp.log(nf / max_exact) / math.log(REL_MAX_DIST / max_exact)
                         * (half - max_exact)).astype(jnp.int32)
    large = jnp.minimum(large, half - 1)
    return ret + jnp.where(n < max_exact, n, large)


def _bias_tables(rel_bias, nheads):
    tile = ATTN_TILE
    qp = jnp.arange(tile)[:, None]
    kp = jnp.arange(tile)[None, :]
    tiles = []
    for off in (-tile, 0):
        b = rel_bias[_rel_bucket(kp + off - qp)].astype(F32)
        b = b.reshape(tile, tile, nheads, 2).transpose(2, 3, 0, 1) * LOG2E
        if off == 0:
            b = jnp.where(kp // CHUNK <= qp // CHUNK, b, NEG_INF)
        tiles.append(b)
    far = rel_bias[_rel_bucket(jnp.int32(-(tile + 1)))].astype(F32) * LOG2E
    return jnp.stack(tiles, axis=2), far


def _attn_kernel(lam_ref, far_ref, q_ref, k_ref, v_ref, bias_ref, sw_ref, o_ref,
                 m_ref, l_ref, acc_ref, *, out_scale):
    tile = ATTN_TILE
    h = pl.program_id(1)
    i = pl.program_id(2)
    dh = HEAD // 2

    m_ref[...] = jnp.full_like(m_ref, NEG_INF)
    l_ref[...] = jnp.zeros_like(l_ref)
    acc_ref[...] = jnp.zeros_like(acc_ref)

    q = q_ref[...]
    lane = lax.broadcasted_iota(jnp.int32, q.shape, 1)
    zero = jnp.zeros_like(q)
    q_maps = (jnp.where(lane < dh, q, zero), jnp.where(lane >= dh, q, zero))

    def step(m, kblk, vblk, bias_tile, far_c):
        s = lax.dot_general(q_maps[m], kblk, (((1,), (1,)), ((), ())), preferred_element_type=F32)
        if bias_tile is not None:
            s = s + bias_tile
        m_cur = jnp.max(s, axis=1, keepdims=True)
        if far_c is not None:
            m_cur = m_cur + far_c
        m_prev = m_ref[m]
        m_new = jnp.maximum(m_prev, m_cur)
        alpha = jnp.exp2(m_prev - m_new)
        shift = m_new[:, :1]
        if far_c is not None:
            shift = shift - far_c
        p = jnp.exp2(s - shift)
        l_ref[m] = alpha * l_ref[m] + jnp.sum(p, axis=1, keepdims=True)
        acc_ref[m] = alpha * acc_ref[m] + jnp.dot(p.astype(BF16), vblk, preferred_element_type=F32)
        m_ref[m] = m_new

    def far_body(j, carry):
        start = pl.multiple_of(j * tile, tile)
        kblk = k_ref[pl.ds(start, tile), :]
        vblk = v_ref[pl.ds(start, tile), :]
        for m in range(2):
            step(m, kblk, vblk, None, far_ref[2 * h + m])
        return carry

    lax.fori_loop(0, jnp.maximum(i - 1, 0), far_body, 0)

    @pl.when(i > 0)
    def _():
        start = pl.multiple_of((i - 1) * tile, tile)
        kblk = k_ref[pl.ds(start, tile), :]
        vblk = v_ref[pl.ds(start, tile), :]
        for m in range(2):
            step(m, kblk, vblk, bias_ref[0, m, 0], None)

    start = pl.multiple_of(i * tile, tile)
    kblk = k_ref[pl.ds(start, tile), :]
    vblk = v_ref[pl.ds(start, tile), :]
    for m in range(2):
        step(m, kblk, vblk, bias_ref[0, m, 1], None)

    o = acc_ref[0] / l_ref[0] - lam_ref[0] * (acc_ref[1] / l_ref[1])
    o_ref[...] = (_rms(o, sw_ref[...]) * out_scale).astype(BF16)


def _diff_attn(q, k, v, bias_tiles, far, lam, subln_w, batch, out_scale):
    t, d = q.shape
    seq = t // batch
    nheads = d // HEAD
    tile = ATTN_TILE
    nq = seq // tile
    smem = pl.BlockSpec(memory_space=pltpu.SMEM)
    q_spec = pl.BlockSpec((tile, HEAD), lambda b, h, i: (b * nq + i, h))
    kv_spec = pl.BlockSpec((seq, HEAD), lambda b, h, i: (b, h))
    return pl.pallas_call(
        functools.partial(_attn_kernel, out_scale=out_scale),
        grid=(batch, nheads, nq),
        in_specs=[smem, smem, q_spec, kv_spec, kv_spec,
                  pl.BlockSpec((1, 2, 2, tile, tile), lambda b, h, i: (h, 0, 0, 0, 0)),
                  pl.BlockSpec((1, HEAD), lambda b, h, i: (0, 0))],
        out_specs=q_spec,
        out_shape=jax.ShapeDtypeStruct((t, d), BF16),
        scratch_shapes=[pltpu.VMEM((2, tile, HEAD), F32)] * 3,
        compiler_params=_params("parallel", "parallel", "arbitrary"),
        name="diff_attn",
    )(lam.reshape(1).astype(F32), far, q, k, v, bias_tiles, subln_w.reshape(1, HEAD).astype(F32))


def kernel(x, a_norm_pre, a_norm_post, a_w_in, a_lb, a_gate_norm, a_w_out, kv_norm, w_kv,
           b_norm_pre, b_norm_post, b_w_q, b_lambda, b_subln, b_w_out, rel_bias,
           mlp_norm_pre, mlp_norm_post, mlp_w_up, mlp_w_down):
    batch, seq, d = x.shape
    n_a = a_w_in.shape[0]
    n_b = b_w_q.shape[0]
    nheads = d // HEAD
    bf = lambda w: w.astype(BF16)

    lb_all = jnp.cumsum(jax.nn.softmax(a_lb.astype(F32), axis=0), axis=0)
    lb_all = lb_all - lb_all[0:1]

    h = x.reshape(batch * seq, d)
    for a in range(n_a):
        q, lf, k, v, sg = _hgrn_in(h, a_norm_pre[a], bf(a_w_in[a]), lb_all[a])
        o = _hgrn_rec(q, lf, k, v, sg, a_gate_norm[a], batch)
        h = _out_proj(o, bf(a_w_out[a]), h, a_norm_post[a])
        h = _mlp(h, mlp_norm_pre[a], bf(mlp_w_up[a]), bf(mlp_w_down[a]), mlp_norm_post[a])

    k_sh, v_sh = _norm_mm(h, kv_norm, bf(w_kv))
    bias_tiles, far = _bias_tables(rel_bias, nheads)
    q_scale = (HEAD // 2) ** -0.5 * LOG2E
    for bi in range(n_b):
        layer = n_a + bi
        lam_init = 0.8 - 0.6 * math.exp(-0.3 * layer)
        lp = b_lambda[bi].astype(F32)
        lam = jnp.exp(jnp.sum(lp[0] * lp[1])) - jnp.exp(jnp.sum(lp[2] * lp[3])) + lam_init
        (q,) = _norm_mm(h, b_norm_pre[bi], bf(b_w_q[bi]), scale=q_scale)
        o = _diff_attn(q, k_sh, v_sh, bias_tiles, far, lam, b_subln[bi], batch, 1.0 - lam_init)
        h = _out_proj(o, bf(b_w_out[bi]), h, b_norm_post[bi])
        h = _mlp(h, mlp_norm_pre[layer], bf(mlp_w_up[layer]), bf(mlp_w_down[layer]), mlp_norm_post[layer])
    return h.reshape(batch, seq, d)
```

```python
import functools
import math

import jax
import jax.numpy as jnp
from jax import lax
from jax.experimental import pallas as pl
from jax.experimental.pallas import tpu as pltpu

EPS = 1e-6
NEG_INF = -1e30
CHUNK = 64
HEAD = 128
REL_BUCKETS = 32
REL_MAX_DIST = 128
LOG2E = math.log2(math.e)

ROW_TILE = 512
REC_ROWS = 512
REC_SUB = 16
ATTN_TILE = 256
VMEM_LIMIT = 56 * 1024 * 1024

F32 = jnp.float32
BF16 = jnp.bfloat16


def _rms(x, w):
    return x * lax.rsqrt(jnp.mean(x * x, axis=-1, keepdims=True) + EPS) * w


def _sigmoid(x):
    return 1.0 / (1.0 + jnp.exp(-x))


def _params(*sem):
    return pltpu.CompilerParams(dimension_semantics=sem, vmem_limit_bytes=VMEM_LIMIT)


def _resident(shape):
    return pl.BlockSpec(shape, lambda *_: (0,) * len(shape), pipeline_mode=pl.Buffered(1))


def _hgrn_in_kernel(h_ref, nw_ref, w_ref, loglb_ref, log1mlb_ref, onemlb_ref,
                    q_ref, lf_ref, k_ref, v_ref, sg_ref):
    d = h_ref.shape[1]
    xn = _rms(h_ref[...], nw_ref[...]).astype(BF16)

    def proj(c):
        return jnp.dot(xn, w_ref[:, c * d:(c + 1) * d], preferred_element_type=F32)

    q = proj(0)
    q_ref[...] = (q * _sigmoid(q)).astype(BF16)
    f = proj(1)
    log_sig = jnp.minimum(f, 0.0) - jnp.log1p(jnp.exp(-jnp.abs(f)))
    a = loglb_ref[...]
    c = log1mlb_ref[...] + log_sig
    lf_ref[...] = jnp.maximum(a, c) + jnp.log1p(jnp.exp(-jnp.abs(a - c)))
    k_ref[...] = (onemlb_ref[...] / (1.0 + jnp.exp(f))).astype(BF16)
    v_ref[...] = proj(2).astype(BF16)
    g = proj(3)
    sg_ref[...] = (g * _sigmoid(g)).astype(BF16)


def _hgrn_in(h, nw, w, lb):
    t, d = h.shape
    row = lambda i: (i, 0)
    vec = lambda a: a.reshape(1, d).astype(F32)
    out_bf = jax.ShapeDtypeStruct((t, d), BF16)
    return pl.pallas_call(
        _hgrn_in_kernel,
        grid=(t // ROW_TILE,),
        in_specs=[pl.BlockSpec((ROW_TILE, d), row), _resident((1, d)), _resident(w.shape),
                  _resident((1, d)), _resident((1, d)), _resident((1, d))],
        out_specs=[pl.BlockSpec((ROW_TILE, d), row)] * 5,
        out_shape=[out_bf, jax.ShapeDtypeStruct((t, d), F32), out_bf, out_bf, out_bf],
        compiler_params=_params("parallel"),
        name="hgrn_in",
    )(h, vec(nw), w, vec(jnp.log(lb)), vec(jnp.log1p(-lb)), vec(1.0 - lb))


def _hgrn_rec_kernel(q_ref, lf_ref, k_ref, v_ref, sg_ref, gnw_ref, o_ref, st_ref, *, nheads):
    sub = REC_SUB

    @pl.when(pl.program_id(1) == 0)
    def _():
        st_ref[...] = jnp.zeros_like(st_ref)

    r_i = lax.broadcasted_iota(jnp.int32, (sub, sub), 0)
    c_i = lax.broadcasted_iota(jnp.int32, (sub, sub), 1)
    cum = (c_i <= r_i).astype(F32)
    ones_w = jnp.ones((HEAD, HEAD), BF16)
    rows = lax.broadcasted_iota(jnp.int32, (sub, HEAD), 0)
    gnw = gnw_ref[...]

    def body(blk, carry):
        r0 = pl.multiple_of(blk * sub, sub)
        for h in range(nheads):
            cs = slice(h * HEAD, (h + 1) * HEAD)
            g = lf_ref[pl.ds(r0, sub), cs]
            b = jnp.dot(cum, g, precision=lax.Precision.HIGHEST, preferred_element_type=F32)
            q = q_ref[pl.ds(r0, sub), cs].astype(F32)
            k = k_ref[pl.ds(r0, sub), cs].astype(F32)
            v = v_ref[pl.ds(r0, sub), cs]
            vf = v.astype(F32)
            b_last = b[sub - 1:sub, :]
            st = st_ref[h]
            qe = (q * jnp.exp(b)).astype(BF16)
            o = lax.dot_general(qe, st.astype(BF16), (((1,), (1,)), ((), ())),
                                preferred_element_type=F32)
            kd = (k * jnp.exp(b_last - b)).astype(BF16)
            upd = lax.dot_general(v, kd, (((0,), (0,)), ((), ())), preferred_element_type=F32)
            st_ref[h] = st * jnp.exp(b_last) + upd
            pieces = []
            for s in range(sub):
                decay = jnp.exp(jnp.minimum(b - b[s:s + 1, :], 0.0))
                a = jnp.where(rows >= s, q * k[s:s + 1, :] * decay, 0.0)
                pieces.append(a.astype(BF16))
            scores = jnp.dot(jnp.concatenate(pieces, axis=0), ones_w, preferred_element_type=F32)
            for s in range(sub):
                o = o + scores[s * sub:(s + 1) * sub, :] * vf[s:s + 1, :]
            og = _rms(o, gnw) * sg_ref[pl.ds(r0, sub), cs].astype(F32)
            o_ref[pl.ds(r0, sub), cs] = og.astype(BF16)
        return carry

    lax.fori_loop(0, q_ref.shape[0] // sub, body, 0)


def _hgrn_rec(q, lf, k, v, sg, gnw, batch):
    t, d = q.shape
    nheads = d // HEAD
    nt = t // batch // REC_ROWS
    row = lambda b, i: (b * nt + i, 0)
    blk = pl.BlockSpec((REC_ROWS, d), row)
    return pl.pallas_call(
        functools.partial(_hgrn_rec_kernel, nheads=nheads),
        grid=(batch, nt),
        in_specs=[blk, blk, blk, blk, blk, _resident((1, HEAD))],
        out_specs=blk,
        out_shape=jax.ShapeDtypeStruct((t, d), BF16),
        scratch_shapes=[pltpu.VMEM((nheads, HEAD, HEAD), F32)],
        compiler_params=_params("parallel", "arbitrary"),
        name="hgrn_rec",
    )(q, lf, k, v, sg, gnw.reshape(1, HEAD).astype(F32))


def _out_proj_kernel(x_ref, w_ref, h_ref, nw_ref, o_ref):
    y = jnp.dot(x_ref[...], w_ref[...], preferred_element_type=F32)
    o_ref[...] = h_ref[...] + _rms(y, nw_ref[...])


def _out_proj(x, w, h, nw):
    t, d = h.shape
    row = lambda i: (i, 0)
    return pl.pallas_call(
        _out_proj_kernel,
        grid=(t // ROW_TILE,),
        in_specs=[pl.BlockSpec((ROW_TILE, d), row), _resident(w.shape),
                  pl.BlockSpec((ROW_TILE, d), row), _resident((1, d))],
        out_specs=pl.BlockSpec((ROW_TILE, d), row),
        out_shape=jax.ShapeDtypeStruct((t, d), F32),
        compiler_params=_params("parallel"),
        name="out_proj",
    )(x, w, h, nw.reshape(1, d).astype(F32))


def _mlp_kernel(h_ref, n1_ref, wu_ref, wd_ref, n2_ref, o_ref, u_ref):
    h = h_ref[...]
    d = h.shape[1]
    xn = _rms(h, n1_ref[...]).astype(BF16)
    for c in range(wu_ref.shape[1] // d):
        u = jnp.maximum(jnp.dot(xn, wu_ref[:, c * d:(c + 1) * d], preferred_element_type=F32), 0.0)
        u_ref[:, c * d:(c + 1) * d] = (u * u).astype(BF16)
    y = jnp.dot(u_ref[...], wd_ref[...], preferred_element_type=F32)
    o_ref[...] = h + _rms(y, n2_ref[...])


def _mlp(h, n1, wu, wd, n2):
    t, d = h.shape
    row = lambda i: (i, 0)
    vec = lambda a: a.reshape(1, d).astype(F32)
    return pl.pallas_call(
        _mlp_kernel,
        grid=(t // ROW_TILE,),
        in_specs=[pl.BlockSpec((ROW_TILE, d), row), _resident((1, d)), _resident(wu.shape),
                  _resident(wd.shape), _resident((1, d))],
        out_specs=pl.BlockSpec((ROW_TILE, d), row),
        out_shape=jax.ShapeDtypeStruct((t, d), F32),
        scratch_shapes=[pltpu.VMEM((ROW_TILE, wu.shape[1]), BF16)],
        compiler_params=_params("parallel"),
        name="mlp",
    )(h, vec(n1), wu, wd, vec(n2))


def _norm_mm_kernel(h_ref, nw_ref, w_ref, *o_refs, scale):
    d = h_ref.shape[1]
    xn = _rms(h_ref[...], nw_ref[...]).astype(BF16)
    for c, o_ref in enumerate(o_refs):
        y = jnp.dot(xn, w_ref[:, c * d:(c + 1) * d], preferred_element_type=F32)
        o_ref[...] = (y * scale).astype(BF16)


def _norm_mm(h, nw, w, scale=1.0):
    t, d = h.shape
    n_out = w.shape[1] // d
    row = lambda i: (i, 0)
    return pl.pallas_call(
        functools.partial(_norm_mm_kernel, scale=scale),
        grid=(t // ROW_TILE,),
        in_specs=[pl.BlockSpec((ROW_TILE, d), row), _resident((1, d)), _resident(w.shape)],
        out_specs=[pl.BlockSpec((ROW_TILE, d), row)] * n_out,
        out_shape=[jax.ShapeDtypeStruct((t, d), BF16)] * n_out,
        compiler_params=_params("parallel"),
        name="norm_mm",
    )(h, nw.reshape(1, d).astype(F32), w)


def _rel_bucket(rel):
    half = REL_BUCKETS // 2
    max_exact = half // 2
    ret = jnp.where(rel > 0, half, 0)
    n = jnp.abs(rel)
    nf = jnp.maximum(n, 1).astype(jnp.float32)
    large = max_exact + (jnp.log(nf / max_exact) / math.log(REL_MAX_DIST / max_exact)
                         * (half - max_exact)).astype(jnp.int32)
    large = jnp.minimum(large, half - 1)
    return ret + jnp.where(n < max_exact, n, large)


def _bias_tables(rel_bias, nheads):
    tile = ATTN_TILE
    kp = jnp.arange(tile)[:, None]
    qp = jnp.arange(tile)[None, :]
    tiles = []
    for off in (-tile, 0):
        b = rel_bias[_rel_bucket(kp + off - qp)].astype(F32)
        b = b.reshape(tile, tile, nheads, 2).transpose(2, 3, 0, 1) * LOG2E
        if off == 0:
            b = jnp.where(kp // CHUNK <= qp // CHUNK, b, NEG_INF)
        tiles.append(b)
    far = rel_bias[_rel_bucket(jnp.int32(-(tile + 1)))].astype(F32) * LOG2E
    return jnp.stack(tiles, axis=2), far


def _attn_kernel(lam_ref, far_ref, q_ref, k_ref, vt_ref, bias_ref, sw_ref, o_ref,
                 m_ref, l_ref, alpha_ref, acc_ref, s_ref, p_ref, *, out_scale):
    h = pl.program_id(1)
    i = pl.program_id(2)
    dh = HEAD // 2

    q = q_ref[...]
    lane = lax.broadcasted_iota(jnp.int32, q.shape, 1)
    zero = jnp.zeros_like(q)
    q_maps = (jnp.where(lane < dh, q, zero), jnp.where(lane >= dh, q, zero))

    def logits(j):
        start = pl.multiple_of(j * ATTN_TILE, ATTN_TILE)
        kblk = k_ref[pl.ds(start, ATTN_TILE), :]
        for m in range(2):
            s_ref[m] = lax.dot_general(kblk, q_maps[m], (((1,), (1,)), ((), ())),
                                       preferred_element_type=F32)

    def softmax(bias_slot):
        for m in range(2):
            s = s_ref[m]
            if bias_slot is None:
                far_c = far_ref[2 * h + m]
                m_cur = jnp.max(s, axis=0, keepdims=True) + far_c
            else:
                s = s + bias_ref[0, m, bias_slot]
                m_cur = jnp.max(s, axis=0, keepdims=True)
            m_prev = m_ref[m]
            m_new = jnp.maximum(m_prev, m_cur)
            alpha = jnp.exp2(m_prev - m_new)
            shift = m_new - far_c if bias_slot is None else m_new
            p = jnp.exp2(s - shift)
            l_ref[m] = alpha * l_ref[m] + jnp.sum(p, axis=0, keepdims=True)
            p_ref[m] = p.astype(BF16)
            alpha_ref[m] = alpha
            m_ref[m] = m_new

    def values(j):
        vtblk = vt_ref[0, jnp.maximum(j, 0)]
        for m in range(2):
            acc_ref[m] = alpha_ref[m] * acc_ref[m] + jnp.dot(vtblk, p_ref[m], preferred_element_type=F32)

    m_ref[...] = jnp.full_like(m_ref, NEG_INF)
    l_ref[...] = jnp.zeros_like(l_ref)
    alpha_ref[...] = jnp.ones_like(alpha_ref)
    acc_ref[...] = jnp.zeros_like(acc_ref)
    p_ref[...] = jnp.zeros_like(p_ref)
    logits(0)

    def far_body(t, carry):
        values(t - 1)
        softmax(None)
        logits(t + 1)
        return carry

    lax.fori_loop(0, jnp.maximum(i - 1, 0), far_body, 0)

    @pl.when(i > 0)
    def _():
        values(i - 2)
        softmax(0)
        logits(i)

    values(i - 1)
    softmax(1)
    values(i)

    o_t = acc_ref[0] / l_ref[0] - lam_ref[0] * (acc_ref[1] / l_ref[1])
    o_ref[...] = (_rms(o_t.T, sw_ref[...]) * out_scale).astype(BF16)


def _diff_attn(q, k, vt, bias_tiles, far, lam, subln_w, batch, out_scale):
    t, d = q.shape
    seq = t // batch
    nheads = d // HEAD
    tile = ATTN_TILE
    nq = seq // tile
    smem = pl.BlockSpec(memory_space=pltpu.SMEM)
    q_spec = pl.BlockSpec((tile, HEAD), lambda b, h, i: (b * nq + i, h))
    return pl.pallas_call(
        functools.partial(_attn_kernel, out_scale=out_scale),
        grid=(batch, nheads, nq),
        in_specs=[smem, smem, q_spec,
                  pl.BlockSpec((seq, HEAD), lambda b, h, i: (b, h)),
                  pl.BlockSpec((1, nq, HEAD, tile), lambda b, h, i: (b * nheads + h, 0, 0, 0)),
                  pl.BlockSpec((1, 2, 2, tile, tile), lambda b, h, i: (h, 0, 0, 0, 0)),
                  pl.BlockSpec((1, HEAD), lambda b, h, i: (0, 0))],
        out_specs=q_spec,
        out_shape=jax.ShapeDtypeStruct((t, d), BF16),
        scratch_shapes=[pltpu.VMEM((2, 1, tile), F32)] * 3
        + [pltpu.VMEM((2, HEAD, tile), F32), pltpu.VMEM((2, tile, tile), F32), pltpu.VMEM((2, tile, tile), BF16)],
        compiler_params=_params("parallel", "parallel", "arbitrary"),
        name="diff_attn",
    )(lam.reshape(1).astype(F32), far, q, k, vt, bias_tiles, subln_w.reshape(1, HEAD).astype(F32))


def kernel(x, a_norm_pre, a_norm_post, a_w_in, a_lb, a_gate_norm, a_w_out, kv_norm, w_kv,
           b_norm_pre, b_norm_post, b_w_q, b_lambda, b_subln, b_w_out, rel_bias,
           mlp_norm_pre, mlp_norm_post, mlp_w_up, mlp_w_down):
    batch, seq, d = x.shape
    n_a = a_w_in.shape[0]
    n_b = b_w_q.shape[0]
    nheads = d // HEAD
    bf = lambda w: w.astype(BF16)

    lb_all = jnp.cumsum(jax.nn.softmax(a_lb.astype(F32), axis=0), axis=0)
    lb_all = lb_all - lb_all[0:1]

    h = x.reshape(batch * seq, d)
    for a in range(n_a):
        q, lf, k, v, sg = _hgrn_in(h, a_norm_pre[a], bf(a_w_in[a]), lb_all[a])
        o = _hgrn_rec(q, lf, k, v, sg, a_gate_norm[a], batch)
        h = _out_proj(o, bf(a_w_out[a]), h, a_norm_post[a])
        h = _mlp(h, mlp_norm_pre[a], bf(mlp_w_up[a]), bf(mlp_w_down[a]), mlp_norm_post[a])

    k_sh, v_sh = _norm_mm(h, kv_norm, bf(w_kv))
    vt = v_sh.reshape(batch, seq // ATTN_TILE, ATTN_TILE, nheads, HEAD).transpose(0, 3, 1, 4, 2)
    vt = vt.reshape(batch * nheads, seq // ATTN_TILE, HEAD, ATTN_TILE)
    bias_tiles, far = _bias_tables(rel_bias, nheads)
    q_scale = (HEAD // 2) ** -0.5 * LOG2E
    for bi in range(n_b):
        layer = n_a + bi
        lam_init = 0.8 - 0.6 * math.exp(-0.3 * layer)
        lp = b_lambda[bi].astype(F32)
        lam = jnp.exp(jnp.sum(lp[0] * lp[1])) - jnp.exp(jnp.sum(lp[2] * lp[3])) + lam_init
        (q,) = _norm_mm(h, b_norm_pre[bi], bf(b_w_q[bi]), scale=q_scale)
        o = _diff_attn(q, k_sh, vt, bias_tiles, far, lam, b_subln[bi], batch, 1.0 - lam_init)
        h = _out_proj(o, bf(b_w_out[bi]), h, b_norm_post[bi])
        h = _mlp(h, mlp_norm_pre[layer], bf(mlp_w_up[layer]), bf(mlp_w_down[layer]), mlp_norm_post[layer])
    return h.reshape(batch, seq, d)
```

```python
import functools
import math

import jax
import jax.numpy as jnp
from jax import lax
from jax.experimental import pallas as pl
from jax.experimental.pallas import tpu as pltpu

EPS = 1e-6
NEG_INF = -1e30
CHUNK = 64
HEAD = 128
REL_BUCKETS = 32
REL_MAX_DIST = 128
LOG2E = math.log2(math.e)

ROW_TILE = 512
REC_ROWS = 512
REC_SUB = 16
ATTN_TILE = 256
ATTN_GROUP = 4
BIAS_NEAR, BIAS_DIAG, BIAS_FAR, BIAS_MASKED = range(4)
VMEM_LIMIT = 56 * 1024 * 1024

F32 = jnp.float32
BF16 = jnp.bfloat16


def _rms(x, w):
    return x * lax.rsqrt(jnp.mean(x * x, axis=-1, keepdims=True) + EPS) * w


def _sigmoid(x):
    return 1.0 / (1.0 + jnp.exp(-x))


def _params(*sem):
    return pltpu.CompilerParams(dimension_semantics=sem, vmem_limit_bytes=VMEM_LIMIT)


def _resident(shape):
    return pl.BlockSpec(shape, lambda *_: (0,) * len(shape), pipeline_mode=pl.Buffered(1))


def _hgrn_in_kernel(h_ref, nw_ref, w_ref, loglb_ref, log1mlb_ref, onemlb_ref,
                    q_ref, lf_ref, k_ref, v_ref, sg_ref):
    d = h_ref.shape[1]
    xn = _rms(h_ref[...], nw_ref[...]).astype(BF16)

    def proj(c):
        return jnp.dot(xn, w_ref[:, c * d:(c + 1) * d], preferred_element_type=F32)

    q = proj(0)
    q_ref[...] = (q * _sigmoid(q)).astype(BF16)
    f = proj(1)
    log_sig = jnp.minimum(f, 0.0) - jnp.log1p(jnp.exp(-jnp.abs(f)))
    a = loglb_ref[...]
    c = log1mlb_ref[...] + log_sig
    lf_ref[...] = jnp.maximum(a, c) + jnp.log1p(jnp.exp(-jnp.abs(a - c)))
    k_ref[...] = (onemlb_ref[...] / (1.0 + jnp.exp(f))).astype(BF16)
    v_ref[...] = proj(2).astype(BF16)
    g = proj(3)
    sg_ref[...] = (g * _sigmoid(g)).astype(BF16)


def _hgrn_in(h, nw, w, lb):
    t, d = h.shape
    row = lambda i: (i, 0)
    vec = lambda a: a.reshape(1, d).astype(F32)
    out_bf = jax.ShapeDtypeStruct((t, d), BF16)
    return pl.pallas_call(
        _hgrn_in_kernel,
        grid=(t // ROW_TILE,),
        in_specs=[pl.BlockSpec((ROW_TILE, d), row), _resident((1, d)), _resident(w.shape),
                  _resident((1, d)), _resident((1, d)), _resident((1, d))],
        out_specs=[pl.BlockSpec((ROW_TILE, d), row)] * 5,
        out_shape=[out_bf, jax.ShapeDtypeStruct((t, d), F32), out_bf, out_bf, out_bf],
        compiler_params=_params("parallel"),
        name="hgrn_in",
    )(h, vec(nw), w, vec(jnp.log(lb)), vec(jnp.log1p(-lb)), vec(1.0 - lb))


def _hgrn_rec_kernel(q_ref, lf_ref, k_ref, v_ref, sg_ref, gnw_ref, o_ref, st_ref, *, nheads):
    sub = REC_SUB

    @pl.when(pl.program_id(1) == 0)
    def _():
        st_ref[...] = jnp.zeros_like(st_ref)

    r_i = lax.broadcasted_iota(jnp.int32, (sub, sub), 0)
    c_i = lax.broadcasted_iota(jnp.int32, (sub, sub), 1)
    cum = (c_i <= r_i).astype(F32)
    ones_w = jnp.ones((HEAD, HEAD), BF16)
    rows = lax.broadcasted_iota(jnp.int32, (sub, HEAD), 0)
    gnw = gnw_ref[...]

    def body(blk, carry):
        r0 = pl.multiple_of(blk * sub, sub)
        for h in range(nheads):
            cs = slice(h * HEAD, (h + 1) * HEAD)
            g = lf_ref[pl.ds(r0, sub), cs]
            b = jnp.dot(cum, g, precision=lax.Precision.HIGHEST, preferred_element_type=F32)
            q = q_ref[pl.ds(r0, sub), cs].astype(F32)
            k = k_ref[pl.ds(r0, sub), cs].astype(F32)
            v = v_ref[pl.ds(r0, sub), cs]
            vf = v.astype(F32)
            b_last = b[sub - 1:sub, :]
            st = st_ref[h]
            qe = (q * jnp.exp(b)).astype(BF16)
            o = lax.dot_general(qe, st.astype(BF16), (((1,), (1,)), ((), ())),
                                preferred_element_type=F32)
            kd = (k * jnp.exp(b_last - b)).astype(BF16)
            upd = lax.dot_general(v, kd, (((0,), (0,)), ((), ())), preferred_element_type=F32)
            st_ref[h] = st * jnp.exp(b_last) + upd
            pieces = []
            for s in range(sub):
                decay = jnp.exp(jnp.minimum(b - b[s:s + 1, :], 0.0))
                a = jnp.where(rows >= s, q * k[s:s + 1, :] * decay, 0.0)
                pieces.append(a.astype(BF16))
            scores = jnp.dot(jnp.concatenate(pieces, axis=0), ones_w, preferred_element_type=F32)
            for s in range(sub):
                o = o + scores[s * sub:(s + 1) * sub, :] * vf[s:s + 1, :]
            og = _rms(o, gnw) * sg_ref[pl.ds(r0, sub), cs].astype(F32)
            o_ref[pl.ds(r0, sub), cs] = og.astype(BF16)
        return carry

    lax.fori_loop(0, q_ref.shape[0] // sub, body, 0)


def _hgrn_rec(q, lf, k, v, sg, gnw, batch):
    t, d = q.shape
    nheads = d // HEAD
    nt = t // batch // REC_ROWS
    row = lambda b, i: (b * nt + i, 0)
    blk = pl.BlockSpec((REC_ROWS, d), row)
    return pl.pallas_call(
        functools.partial(_hgrn_rec_kernel, nheads=nheads),
        grid=(batch, nt),
        in_specs=[blk, blk, blk, blk, blk, _resident((1, HEAD))],
        out_specs=blk,
        out_shape=jax.ShapeDtypeStruct((t, d), BF16),
        scratch_shapes=[pltpu.VMEM((nheads, HEAD, HEAD), F32)],
        compiler_params=_params("parallel", "arbitrary"),
        name="hgrn_rec",
    )(q, lf, k, v, sg, gnw.reshape(1, HEAD).astype(F32))


def _out_proj_kernel(x_ref, w_ref, h_ref, nw_ref, o_ref):
    y = jnp.dot(x_ref[...], w_ref[...], preferred_element_type=F32)
    o_ref[...] = h_ref[...] + _rms(y, nw_ref[...])


def _out_proj(x, w, h, nw):
    t, d = h.shape
    row = lambda i: (i, 0)
    return pl.pallas_call(
        _out_proj_kernel,
        grid=(t // ROW_TILE,),
        in_specs=[pl.BlockSpec((ROW_TILE, d), row), _resident(w.shape),
                  pl.BlockSpec((ROW_TILE, d), row), _resident((1, d))],
        out_specs=pl.BlockSpec((ROW_TILE, d), row),
        out_shape=jax.ShapeDtypeStruct((t, d), F32),
        compiler_params=_params("parallel"),
        name="out_proj",
    )(x, w, h, nw.reshape(1, d).astype(F32))


def _mlp_kernel(h_ref, n1_ref, wu_ref, wd_ref, n2_ref, o_ref, u_ref):
    h = h_ref[...]
    d = h.shape[1]
    xn = _rms(h, n1_ref[...]).astype(BF16)
    for c in range(wu_ref.shape[1] // d):
        u = jnp.maximum(jnp.dot(xn, wu_ref[:, c * d:(c + 1) * d], preferred_element_type=F32), 0.0)
        u_ref[:, c * d:(c + 1) * d] = (u * u).astype(BF16)
    y = jnp.dot(u_ref[...], wd_ref[...], preferred_element_type=F32)
    o_ref[...] = h + _rms(y, n2_ref[...])


def _mlp(h, n1, wu, wd, n2):
    t, d = h.shape
    row = lambda i: (i, 0)
    vec = lambda a: a.reshape(1, d).astype(F32)
    return pl.pallas_call(
        _mlp_kernel,
        grid=(t // ROW_TILE,),
        in_specs=[pl.BlockSpec((ROW_TILE, d), row), _resident((1, d)), _resident(wu.shape),
                  _resident(wd.shape), _resident((1, d))],
        out_specs=pl.BlockSpec((ROW_TILE, d), row),
        out_shape=jax.ShapeDtypeStruct((t, d), F32),
        scratch_shapes=[pltpu.VMEM((ROW_TILE, wu.shape[1]), BF16)],
        compiler_params=_params("parallel"),
        name="mlp",
    )(h, vec(n1), wu, wd, vec(n2))


def _norm_mm_kernel(h_ref, nw_ref, w_ref, *o_refs, scale):
    d = h_ref.shape[1]
    xn = _rms(h_ref[...], nw_ref[...]).astype(BF16)
    for c, o_ref in enumerate(o_refs):
        y = jnp.dot(xn, w_ref[:, c * d:(c + 1) * d], preferred_element_type=F32)
        o_ref[...] = (y * scale).astype(BF16)


def _norm_mm(h, nw, w, scale=1.0):
    t, d = h.shape
    n_out = w.shape[1] // d
    row = lambda i: (i, 0)
    return pl.pallas_call(
        functools.partial(_norm_mm_kernel, scale=scale),
        grid=(t // ROW_TILE,),
        in_specs=[pl.BlockSpec((ROW_TILE, d), row), _resident((1, d)), _resident(w.shape)],
        out_specs=[pl.BlockSpec((ROW_TILE, d), row)] * n_out,
        out_shape=[jax.ShapeDtypeStruct((t, d), BF16)] * n_out,
        compiler_params=_params("parallel"),
        name="norm_mm",
    )(h, nw.reshape(1, d).astype(F32), w)


def _rel_bucket(rel):
    half = REL_BUCKETS // 2
    max_exact = half // 2
    ret = jnp.where(rel > 0, half, 0)
    n = jnp.abs(rel)
    nf = jnp.maximum(n, 1).astype(jnp.float32)
    large = max_exact + (jnp.log(nf / max_exact) / math.log(REL_MAX_DIST / max_exact)
                         * (half - max_exact)).astype(jnp.int32)
    large = jnp.minimum(large, half - 1)
    return ret + jnp.where(n < max_exact, n, large)


def _bias_tables(rel_bias, nheads):
    tile = ATTN_TILE
    kp = jnp.arange(tile)[:, None]
    qp = jnp.arange(tile)[None, :]
    table = rel_bias.astype(F32) * LOG2E

    def lookup(rel):
        onehot = (_rel_bucket(rel)[..., None] == jnp.arange(REL_BUCKETS)).astype(F32)
        b = jnp.dot(onehot, table, precision=lax.Precision.HIGHEST)
        return b.reshape(tile, tile, nheads, 2).transpose(2, 3, 0, 1)

    far = table[_rel_bucket(jnp.int32(-(tile + 1)))]
    kinds = [None] * 4
    kinds[BIAS_NEAR] = lookup(kp - tile - qp)
    kinds[BIAS_DIAG] = jnp.where(kp // CHUNK <= qp // CHUNK, lookup(kp - qp), NEG_INF)
    kinds[BIAS_FAR] = jnp.broadcast_to(far.reshape(nheads, 2, 1, 1), (nheads, 2, tile, tile))
    kinds[BIAS_MASKED] = jnp.full((nheads, 2, tile, tile), NEG_INF, F32)
    return jnp.stack(kinds, axis=2), far


def _attn_kernel(lam_ref, far_ref, q_ref, k_ref, vt_ref, bias_ref, sw_ref, o_ref,
                 m_ref, l_ref, alpha_ref, acc_ref, s_ref, p_ref, *, out_scale):
    h = pl.program_id(1)
    i = pl.program_id(2)
    dh = HEAD // 2
    group = ATTN_GROUP
    tile = ATTN_TILE
    width = group * tile
    nsteps = i // group + 1

    q = q_ref[...]
    lane = lax.broadcasted_iota(jnp.int32, q.shape, 1)
    zero = jnp.zeros_like(q)
    q_maps = (jnp.where(lane < dh, q, zero), jnp.where(lane >= dh, q, zero))

    def logits(t):
        start = pl.multiple_of(t * width, width)
        kblk = k_ref[pl.ds(start, width), :]
        for m in range(2):
            s_ref[m] = lax.dot_general(kblk, q_maps[m], (((1,), (1,)), ((), ())),
                                       preferred_element_type=F32)

    def tile_kind(j):
        return jnp.where(j <= i - 2, BIAS_FAR,
                         jnp.where(j == i - 1, BIAS_NEAR, jnp.where(j == i, BIAS_DIAG, BIAS_MASKED)))

    def softmax(t, edge):
        for m in range(2):
            s = s_ref[m]
            if edge:
                s = jnp.concatenate(
                    [s[g * tile:(g + 1) * tile] + bias_ref[0, m, tile_kind(t * group + g)]
                     for g in range(group)], axis=0)
                m_cur = jnp.max(s, axis=0, keepdims=True)
            else:
                far_c = far_ref[2 * h + m]
                m_cur = jnp.max(s, axis=0, keepdims=True) + far_c
            m_prev = m_ref[m]
            m_new = jnp.maximum(m_prev, m_cur)
            alpha = jnp.exp2(m_prev - m_new)
            shift = m_new if edge else m_new - far_c
            p = jnp.exp2(s - shift)
            l_ref[m] = alpha * l_ref[m] + jnp.sum(p, axis=0, keepdims=True)
            p_ref[m] = p.astype(BF16)
            alpha_ref[m] = alpha
            m_ref[m] = m_new

    def values(t):
        vtblk = vt_ref[0, jnp.maximum(t, 0)]
        for m in range(2):
            acc_ref[m] = alpha_ref[m] * acc_ref[m] + jnp.dot(vtblk, p_ref[m], preferred_element_type=F32)

    m_ref[...] = jnp.full_like(m_ref, NEG_INF)
    l_ref[...] = jnp.zeros_like(l_ref)
    alpha_ref[...] = jnp.ones_like(alpha_ref)
    acc_ref[...] = jnp.zeros_like(acc_ref)
    p_ref[...] = jnp.zeros_like(p_ref)
    logits(0)

    def far_body(t, carry):
        values(t - 1)
        softmax(t, edge=False)
        logits(t + 1)
        return carry

    lax.fori_loop(0, jnp.maximum(nsteps - 2, 0), far_body, 0)

    @pl.when(nsteps >= 2)
    def _():
        values(nsteps - 3)
        softmax(nsteps - 2, edge=True)
        logits(nsteps - 1)

    values(nsteps - 2)
    softmax(nsteps - 1, edge=True)
    values(nsteps - 1)

    o_t = acc_ref[0] / l_ref[0] - lam_ref[0] * (acc_ref[1] / l_ref[1])
    o_ref[...] = (_rms(o_t.T, sw_ref[...]) * out_scale).astype(BF16)


def _diff_attn(q, k, vt, bias_tiles, far, lam, subln_w, batch, out_scale):
    t, d = q.shape
    seq = t // batch
    nheads = d // HEAD
    tile = ATTN_TILE
    width = ATTN_GROUP * tile
    nq = seq // tile
    smem = pl.BlockSpec(memory_space=pltpu.SMEM)
    q_spec = pl.BlockSpec((tile, HEAD), lambda b, h, i: (b * nq + i, h))
    return pl.pallas_call(
        functools.partial(_attn_kernel, out_scale=out_scale),
        grid=(batch, nheads, nq),
        in_specs=[smem, smem, q_spec,
                  pl.BlockSpec((seq, HEAD), lambda b, h, i: (b, h)),
                  pl.BlockSpec((1, seq // width, HEAD, width), lambda b, h, i: (b * nheads + h, 0, 0, 0)),
                  pl.BlockSpec((1, 2, 4, tile, tile), lambda b, h, i: (h, 0, 0, 0, 0)),
                  pl.BlockSpec((1, HEAD), lambda b, h, i: (0, 0))],
        out_specs=q_spec,
        out_shape=jax.ShapeDtypeStruct((t, d), BF16),
        scratch_shapes=[pltpu.VMEM((2, 1, tile), F32)] * 3
        + [pltpu.VMEM((2, HEAD, tile), F32), pltpu.VMEM((2, width, tile), F32),
           pltpu.VMEM((2, width, tile), BF16)],
        compiler_params=_params("parallel", "parallel", "arbitrary"),
        name="diff_attn",
    )(lam.reshape(1).astype(F32), far, q, k, vt, bias_tiles, subln_w.reshape(1, HEAD).astype(F32))


def kernel(x, a_norm_pre, a_norm_post, a_w_in, a_lb, a_gate_norm, a_w_out, kv_norm, w_kv,
           b_norm_pre, b_norm_post, b_w_q, b_lambda, b_subln, b_w_out, rel_bias,
           mlp_norm_pre, mlp_norm_post, mlp_w_up, mlp_w_down):
    batch, seq, d = x.shape
    n_a = a_w_in.shape[0]
    n_b = b_w_q.shape[0]
    nheads = d // HEAD
    bf = lambda w: w.astype(BF16)

    lb_all = jnp.cumsum(jax.nn.softmax(a_lb.astype(F32), axis=0), axis=0)
    lb_all = lb_all - lb_all[0:1]

    h = x.reshape(batch * seq, d)
    for a in range(n_a):
        q, lf, k, v, sg = _hgrn_in(h, a_norm_pre[a], bf(a_w_in[a]), lb_all[a])
        o = _hgrn_rec(q, lf, k, v, sg, a_gate_norm[a], batch)
        h = _out_proj(o, bf(a_w_out[a]), h, a_norm_post[a])
        h = _mlp(h, mlp_norm_pre[a], bf(mlp_w_up[a]), bf(mlp_w_down[a]), mlp_norm_post[a])

    k_sh, v_sh = _norm_mm(h, kv_norm, bf(w_kv))
    width = ATTN_GROUP * ATTN_TILE
    vt = v_sh.reshape(batch, seq // width, width, nheads, HEAD).transpose(0, 3, 1, 4, 2)
    vt = vt.reshape(batch * nheads, seq // width, HEAD, width)
    bias_tiles, far = _bias_tables(rel_bias, nheads)
    q_scale = (HEAD // 2) ** -0.5 * LOG2E
    for bi in range(n_b):
        layer = n_a + bi
        lam_init = 0.8 - 0.6 * math.exp(-0.3 * layer)
        lp = b_lambda[bi].astype(F32)
        lam = jnp.exp(jnp.sum(lp[0] * lp[1])) - jnp.exp(jnp.sum(lp[2] * lp[3])) + lam_init
        (q,) = _norm_mm(h, b_norm_pre[bi], bf(b_w_q[bi]), scale=q_scale)
        o = _diff_attn(q, k_sh, vt, bias_tiles, far, lam, b_subln[bi], batch, 1.0 - lam_init)
        h = _out_proj(o, bf(b_w_out[bi]), h, b_norm_post[bi])
        h = _mlp(h, mlp_norm_pre[layer], bf(mlp_w_up[layer]), bf(mlp_w_down[layer]), mlp_norm_post[layer])
    return h.reshape(batch, seq, d)
```

```python
import functools
import math

import jax
import jax.numpy as jnp
from jax import lax
from jax.experimental import pallas as pl
from jax.experimental.pallas import tpu as pltpu

EPS = 1e-6
NEG_INF = -1e30
CHUNK = 64
HEAD = 128
REL_BUCKETS = 32
REL_MAX_DIST = 128
LOG2E = math.log2(math.e)

ROW_TILE = 512
REC_ROWS = 512
REC_SUB = 16
ATTN_TILE = 256
ATTN_GROUP = 4
BIAS_NEAR, BIAS_DIAG, BIAS_FAR, BIAS_MASKED = range(4)
VMEM_LIMIT = 56 * 1024 * 1024

F32 = jnp.float32
BF16 = jnp.bfloat16


def _rms(x, w):
    return x * lax.rsqrt(jnp.mean(x * x, axis=-1, keepdims=True) + EPS) * w


def _sigmoid(x):
    return 1.0 / (1.0 + jnp.exp(-x))


def _params(*sem):
    return pltpu.CompilerParams(dimension_semantics=sem, vmem_limit_bytes=VMEM_LIMIT)


def _resident(shape):
    return pl.BlockSpec(shape, lambda *_: (0,) * len(shape), pipeline_mode=pl.Buffered(1))


def _hgrn_in_kernel(h_ref, nw_ref, w_ref, loglb_ref, log1mlb_ref, onemlb_ref,
                    q_ref, lf_ref, k_ref, v_ref, sg_ref):
    d = h_ref.shape[1]
    xn = _rms(h_ref[...], nw_ref[...]).astype(BF16)

    def proj(c):
        return jnp.dot(xn, w_ref[:, c * d:(c + 1) * d], preferred_element_type=F32)

    q = proj(0)
    q_ref[...] = (q * _sigmoid(q)).astype(BF16)
    f = proj(1)
    log_sig = jnp.minimum(f, 0.0) - jnp.log1p(jnp.exp(-jnp.abs(f)))
    a = loglb_ref[...]
    c = log1mlb_ref[...] + log_sig
    log_f = jnp.maximum(a, c) + jnp.log1p(jnp.exp(-jnp.abs(a - c)))
    b = log_f * LOG2E
    row_in_blk = lax.broadcasted_iota(jnp.int32, b.shape, 0) % REC_SUB
    shift = 1
    while shift < REC_SUB:
        b = b + jnp.where(row_in_blk >= shift, pltpu.roll(b, shift, axis=0), 0.0)
        shift *= 2
    lf_ref[...] = b
    k_ref[...] = (onemlb_ref[...] / (1.0 + jnp.exp(f))).astype(BF16)
    v_ref[...] = proj(2).astype(BF16)
    g = proj(3)
    sg_ref[...] = (g * _sigmoid(g)).astype(BF16)


def _hgrn_in(h, nw, w, lb):
    t, d = h.shape
    row = lambda i: (i, 0)
    vec = lambda a: a.reshape(1, d).astype(F32)
    out_bf = jax.ShapeDtypeStruct((t, d), BF16)
    return pl.pallas_call(
        _hgrn_in_kernel,
        grid=(t // ROW_TILE,),
        in_specs=[pl.BlockSpec((ROW_TILE, d), row), _resident((1, d)), _resident(w.shape),
                  _resident((1, d)), _resident((1, d)), _resident((1, d))],
        out_specs=[pl.BlockSpec((ROW_TILE, d), row)] * 5,
        out_shape=[out_bf, jax.ShapeDtypeStruct((t, d), F32), out_bf, out_bf, out_bf],
        compiler_params=_params("parallel"),
        name="hgrn_in",
    )(h, vec(nw), w, vec(jnp.log(lb)), vec(jnp.log1p(-lb)), vec(1.0 - lb))


def _hgrn_rec_kernel(q_ref, b_ref, k_ref, v_ref, sg_ref, gnw_ref, o_ref, st_ref, *, nheads):
    sub = REC_SUB
    half = sub // 2

    @pl.when(pl.program_id(1) == 0)
    def _():
        st_ref[...] = jnp.zeros_like(st_ref)

    ones_w = jnp.ones((HEAD, HEAD), BF16)
    row = lax.broadcasted_iota(jnp.int32, (half, HEAD), 0)
    gnw = gnw_ref[...]
    heads = range(nheads)

    def body(blk, carry):
        r0 = pl.multiple_of(blk * sub, sub)
        cs = [slice(h * HEAD, (h + 1) * HEAD) for h in heads]
        b = [b_ref[pl.ds(r0, sub), cs[h]] for h in heads]
        q = [q_ref[pl.ds(r0, sub), cs[h]].astype(F32) for h in heads]
        k = [k_ref[pl.ds(r0, sub), cs[h]].astype(F32) for h in heads]
        v = [v_ref[pl.ds(r0, sub), cs[h]] for h in heads]

        inter, upd, carry_decay = [], [], []
        for h in heads:
            b_last = b[h][sub - 1:sub, :]
            qe = (q[h] * jnp.exp2(b[h])).astype(BF16)
            kd = (k[h] * jnp.exp2(b_last - b[h])).astype(BF16)
            inter.append(lax.dot_general(qe, st_ref[h].astype(BF16), (((1,), (1,)), ((), ())),
                                         preferred_element_type=F32))
            upd.append(lax.dot_general(v[h], kd, (((0,), (0,)), ((), ())), preferred_element_type=F32))
            carry_decay.append(jnp.exp2(b_last))

        scores = []
        for h in heads:
            lo = slice(0, half)
            hi = slice(half, sub)
            pieces = []
            for s in range(sub):
                b_s = b[h][s:s + 1, :]
                k_s = k[h][s:s + 1, :]
                a_hi = q[h][hi] * k_s * jnp.exp2(jnp.minimum(b[h][hi] - b_s, 0.0))
                if s < half:
                    a_lo = q[h][lo] * k_s * jnp.exp2(jnp.minimum(b[h][lo] - b_s, 0.0))
                    pieces += [jnp.where(row >= s, a_lo, 0.0), a_hi]
                else:
                    pieces.append(jnp.where(row >= s - half, a_hi, 0.0))
            stacked = jnp.concatenate(pieces, axis=0).astype(BF16)
            scores.append(jnp.dot(stacked, ones_w, preferred_element_type=F32))

        for h in heads:
            st_ref[h] = st_ref[h] * carry_decay[h] + upd[h]
            vf = v[h].astype(F32)
            o_lo = inter[h][:half]
            o_hi = inter[h][half:]
            off = 0
            for s in range(sub):
                v_s = vf[s:s + 1, :]
                if s < half:
                    o_lo = o_lo + scores[h][off:off + half] * v_s
                    off += half
                o_hi = o_hi + scores[h][off:off + half] * v_s
                off += half
            o = jnp.concatenate([o_lo, o_hi], axis=0)
            og = _rms(o, gnw) * sg_ref[pl.ds(r0, sub), cs[h]].astype(F32)
            o_ref[pl.ds(r0, sub), cs[h]] = og.astype(BF16)
        return carry

    lax.fori_loop(0, q_ref.shape[0] // sub, body, 0)


def _hgrn_rec(q, lf, k, v, sg, gnw, batch):
    t, d = q.shape
    nheads = d // HEAD
    nt = t // batch // REC_ROWS
    row = lambda b, i: (b * nt + i, 0)
    blk = pl.BlockSpec((REC_ROWS, d), row)
    return pl.pallas_call(
        functools.partial(_hgrn_rec_kernel, nheads=nheads),
        grid=(batch, nt),
        in_specs=[blk, blk, blk, blk, blk, _resident((1, HEAD))],
        out_specs=blk,
        out_shape=jax.ShapeDtypeStruct((t, d), BF16),
        scratch_shapes=[pltpu.VMEM((nheads, HEAD, HEAD), F32)],
        compiler_params=_params("parallel", "arbitrary"),
        name="hgrn_rec",
    )(q, lf, k, v, sg, gnw.reshape(1, HEAD).astype(F32))


def _out_proj_kernel(x_ref, w_ref, h_ref, nw_ref, o_ref):
    y = jnp.dot(x_ref[...], w_ref[...], preferred_element_type=F32)
    o_ref[...] = h_ref[...] + _rms(y, nw_ref[...])


def _out_proj(x, w, h, nw):
    t, d = h.shape
    row = lambda i: (i, 0)
    return pl.pallas_call(
        _out_proj_kernel,
        grid=(t // ROW_TILE,),
        in_specs=[pl.BlockSpec((ROW_TILE, d), row), _resident(w.shape),
                  pl.BlockSpec((ROW_TILE, d), row), _resident((1, d))],
        out_specs=pl.BlockSpec((ROW_TILE, d), row),
        out_shape=jax.ShapeDtypeStruct((t, d), F32),
        compiler_params=_params("parallel"),
        name="out_proj",
    )(x, w, h, nw.reshape(1, d).astype(F32))


def _mlp_kernel(h_ref, n1_ref, wu_ref, wd_ref, n2_ref, o_ref, u_ref):
    h = h_ref[...]
    d = h.shape[1]
    xn = _rms(h, n1_ref[...]).astype(BF16)
    for c in range(wu_ref.shape[1] // d):
        u = jnp.maximum(jnp.dot(xn, wu_ref[:, c * d:(c + 1) * d], preferred_element_type=F32), 0.0)
        u_ref[:, c * d:(c + 1) * d] = (u * u).astype(BF16)
    y = jnp.dot(u_ref[...], wd_ref[...], preferred_element_type=F32)
    o_ref[...] = h + _rms(y, n2_ref[...])


def _mlp(h, n1, wu, wd, n2):
    t, d = h.shape
    row = lambda i: (i, 0)
    vec = lambda a: a.reshape(1, d).astype(F32)
    return pl.pallas_call(
        _mlp_kernel,
        grid=(t // ROW_TILE,),
        in_specs=[pl.BlockSpec((ROW_TILE, d), row), _resident((1, d)), _resident(wu.shape),
                  _resident(wd.shape), _resident((1, d))],
        out_specs=pl.BlockSpec((ROW_TILE, d), row),
        out_shape=jax.ShapeDtypeStruct((t, d), F32),
        scratch_shapes=[pltpu.VMEM((ROW_TILE, wu.shape[1]), BF16)],
        compiler_params=_params("parallel"),
        name="mlp",
    )(h, vec(n1), wu, wd, vec(n2))


def _norm_mm_kernel(h_ref, nw_ref, w_ref, *o_refs, scale):
    d = h_ref.shape[1]
    xn = _rms(h_ref[...], nw_ref[...]).astype(BF16)
    for c, o_ref in enumerate(o_refs):
        y = jnp.dot(xn, w_ref[:, c * d:(c + 1) * d], preferred_element_type=F32)
        o_ref[...] = (y * scale).astype(BF16)


def _norm_mm(h, nw, w, scale=1.0):
    t, d = h.shape
    n_out = w.shape[1] // d
    row = lambda i: (i, 0)
    return pl.pallas_call(
        functools.partial(_norm_mm_kernel, scale=scale),
        grid=(t // ROW_TILE,),
        in_specs=[pl.BlockSpec((ROW_TILE, d), row), _resident((1, d)), _resident(w.shape)],
        out_specs=[pl.BlockSpec((ROW_TILE, d), row)] * n_out,
        out_shape=[jax.ShapeDtypeStruct((t, d), BF16)] * n_out,
        compiler_params=_params("parallel"),
        name="norm_mm",
    )(h, nw.reshape(1, d).astype(F32), w)


def _rel_bucket(rel):
    half = REL_BUCKETS // 2
    max_exact = half // 2
    ret = jnp.where(rel > 0, half, 0)
    n = jnp.abs(rel)
    nf = jnp.maximum(n, 1).astype(jnp.float32)
    large = max_exact + (jnp.log(nf / max_exact) / math.log(REL_MAX_DIST / max_exact)
                         * (half - max_exact)).astype(jnp.int32)
    large = jnp.minimum(large, half - 1)
    return ret + jnp.where(n < max_exact, n, large)


def _bias_tables(rel_bias, nheads):
    tile = ATTN_TILE
    kp = jnp.arange(tile)[:, None]
    qp = jnp.arange(tile)[None, :]
    table = rel_bias.astype(F32) * LOG2E

    def lookup(rel):
        onehot = (_rel_bucket(rel)[..., None] == jnp.arange(REL_BUCKETS)).astype(F32)
        b = jnp.dot(onehot, table, precision=lax.Precision.HIGHEST)
        return b.reshape(tile, tile, nheads, 2).transpose(2, 3, 0, 1)

    far = table[_rel_bucket(jnp.int32(-(tile + 1)))]
    kinds = [None] * 4
    kinds[BIAS_NEAR] = lookup(kp - tile - qp)
    kinds[BIAS_DIAG] = jnp.where(kp // CHUNK <= qp // CHUNK, lookup(kp - qp), NEG_INF)
    kinds[BIAS_FAR] = jnp.broadcast_to(far.reshape(nheads, 2, 1, 1), (nheads, 2, tile, tile))
    kinds[BIAS_MASKED] = jnp.full((nheads, 2, tile, tile), NEG_INF, F32)
    return jnp.stack(kinds, axis=2), far


def _attn_kernel(lam_ref, far_ref, q_ref, k_ref, vt_ref, bias_ref, sw_ref, o_ref,
                 m_ref, l_ref, alpha_ref, acc_ref, s_ref, p_ref, *, out_scale):
    h = pl.program_id(1)
    i = pl.program_id(2)
    dh = HEAD // 2
    group = ATTN_GROUP
    tile = ATTN_TILE
    width = group * tile
    nsteps = i // group + 1

    q = q_ref[...]
    lane = lax.broadcasted_iota(jnp.int32, q.shape, 1)
    zero = jnp.zeros_like(q)
    q_maps = (jnp.where(lane < dh, q, zero), jnp.where(lane >= dh, q, zero))

    def logits(t):
        start = pl.multiple_of(t * width, width)
        kblk = k_ref[pl.ds(start, width), :]
        for m in range(2):
            s_ref[m] = lax.dot_general(kblk, q_maps[m], (((1,), (1,)), ((), ())),
                                       preferred_element_type=F32)

    def tile_kind(j):
        return jnp.where(j <= i - 2, BIAS_FAR,
                         jnp.where(j == i - 1, BIAS_NEAR, jnp.where(j == i, BIAS_DIAG, BIAS_MASKED)))

    def softmax(t, edge):
        for m in range(2):
            s = s_ref[m]
            if edge:
                s = jnp.concatenate(
                    [s[g * tile:(g + 1) * tile] + bias_ref[0, m, tile_kind(t * group + g)]
                     for g in range(group)], axis=0)
                m_cur = jnp.max(s, axis=0, keepdims=True)
            else:
                far_c = far_ref[2 * h + m]
                m_cur = jnp.max(s, axis=0, keepdims=True) + far_c
            m_prev = m_ref[m]
            m_new = jnp.maximum(m_prev, m_cur)
            alpha = jnp.exp2(m_prev - m_new)
            shift = m_new if edge else m_new - far_c
            p = jnp.exp2(s - shift)
            l_ref[m] = alpha * l_ref[m] + jnp.sum(p, axis=0, keepdims=True)
            p_ref[m] = p.astype(BF16)
            alpha_ref[m] = alpha
            m_ref[m] = m_new

    def values(t):
        vtblk = vt_ref[0, jnp.maximum(t, 0)]
        for m in range(2):
            acc_ref[m] = alpha_ref[m] * acc_ref[m] + jnp.dot(vtblk, p_ref[m], preferred_element_type=F32)

    m_ref[...] = jnp.full_like(m_ref, NEG_INF)
    l_ref[...] = jnp.zeros_like(l_ref)
    alpha_ref[...] = jnp.ones_like(alpha_ref)
    acc_ref[...] = jnp.zeros_like(acc_ref)
    p_ref[...] = jnp.zeros_like(p_ref)
    logits(0)

    def far_body(t, carry):
        values(t - 1)
        softmax(t, edge=False)
        logits(t + 1)
        return carry

    lax.fori_loop(0, jnp.maximum(nsteps - 2, 0), far_body, 0)

    @pl.when(nsteps >= 2)
    def _():
        values(nsteps - 3)
        softmax(nsteps - 2, edge=True)
        logits(nsteps - 1)

    values(nsteps - 2)
    softmax(nsteps - 1, edge=True)
    values(nsteps - 1)

    o_t = acc_ref[0] / l_ref[0] - lam_ref[0] * (acc_ref[1] / l_ref[1])
    o_ref[...] = (_rms(o_t.T, sw_ref[...]) * out_scale).astype(BF16)


def _diff_attn(q, k, vt, bias_tiles, far, lam, subln_w, batch, out_scale):
    t, d = q.shape
    seq = t // batch
    nheads = d // HEAD
    tile = ATTN_TILE
    width = ATTN_GROUP * tile
    nq = seq // tile
    smem = pl.BlockSpec(memory_space=pltpu.SMEM)
    q_spec = pl.BlockSpec((tile, HEAD), lambda b, h, i: (b * nq + i, h))
    return pl.pallas_call(
        functools.partial(_attn_kernel, out_scale=out_scale),
        grid=(batch, nheads, nq),
        in_specs=[smem, smem, q_spec,
                  pl.BlockSpec((seq, HEAD), lambda b, h, i: (b, h)),
                  pl.BlockSpec((1, seq // width, HEAD, width), lambda b, h, i: (b * nheads + h, 0, 0, 0)),
                  pl.BlockSpec((1, 2, 4, tile, tile), lambda b, h, i: (h, 0, 0, 0, 0)),
                  pl.BlockSpec((1, HEAD), lambda b, h, i: (0, 0))],
        out_specs=q_spec,
        out_shape=jax.ShapeDtypeStruct((t, d), BF16),
        scratch_shapes=[pltpu.VMEM((2, 1, tile), F32)] * 3
        + [pltpu.VMEM((2, HEAD, tile), F32), pltpu.VMEM((2, width, tile), F32),
           pltpu.VMEM((2, width, tile), BF16)],
        compiler_params=_params("parallel", "parallel", "arbitrary"),
        name="diff_attn",
    )(lam.reshape(1).astype(F32), far, q, k, vt, bias_tiles, subln_w.reshape(1, HEAD).astype(F32))


def kernel(x, a_norm_pre, a_norm_post, a_w_in, a_lb, a_gate_norm, a_w_out, kv_norm, w_kv,
           b_norm_pre, b_norm_post, b_w_q, b_lambda, b_subln, b_w_out, rel_bias,
           mlp_norm_pre, mlp_norm_post, mlp_w_up, mlp_w_down):
    batch, seq, d = x.shape
    n_a = a_w_in.shape[0]
    n_b = b_w_q.shape[0]
    nheads = d // HEAD
    bf = lambda w: w.astype(BF16)

    lb_all = jnp.cumsum(jax.nn.softmax(a_lb.astype(F32), axis=0), axis=0)
    lb_all = lb_all - lb_all[0:1]

    h = x.reshape(batch * seq, d)
    for a in range(n_a):
        q, lf, k, v, sg = _hgrn_in(h, a_norm_pre[a], bf(a_w_in[a]), lb_all[a])
        o = _hgrn_rec(q, lf, k, v, sg, a_gate_norm[a], batch)
        h = _out_proj(o, bf(a_w_out[a]), h, a_norm_post[a])
        h = _mlp(h, mlp_norm_pre[a], bf(mlp_w_up[a]), bf(mlp_w_down[a]), mlp_norm_post[a])

    k_sh, v_sh = _norm_mm(h, kv_norm, bf(w_kv))
    width = ATTN_GROUP * ATTN_TILE
    vt = v_sh.reshape(batch, seq // width, width, nheads, HEAD).transpose(0, 3, 1, 4, 2)
    vt = vt.reshape(batch * nheads, seq // width, HEAD, width)
    bias_tiles, far = _bias_tables(rel_bias, nheads)
    q_scale = (HEAD // 2) ** -0.5 * LOG2E
    for bi in range(n_b):
        layer = n_a + bi
        lam_init = 0.8 - 0.6 * math.exp(-0.3 * layer)
        lp = b_lambda[bi].astype(F32)
        lam = jnp.exp(jnp.sum(lp[0] * lp[1])) - jnp.exp(jnp.sum(lp[2] * lp[3])) + lam_init
        (q,) = _norm_mm(h, b_norm_pre[bi], bf(b_w_q[bi]), scale=q_scale)
        o = _diff_attn(q, k_sh, vt, bias_tiles, far, lam, b_subln[bi], batch, 1.0 - lam_init)
        h = _out_proj(o, bf(b_w_out[bi]), h, b_norm_post[bi])
        h = _mlp(h, mlp_norm_pre[layer], bf(mlp_w_up[layer]), bf(mlp_w_down[layer]), mlp_norm_post[layer])
    return h.reshape(batch, seq, d)
```

```python
import functools
import math

import jax
import jax.numpy as jnp
from jax import lax
from jax.experimental import pallas as pl
from jax.experimental.pallas import tpu as pltpu

EPS = 1e-6
NEG_INF = -1e30
CHUNK = 64
HEAD = 128
REL_BUCKETS = 32
REL_MAX_DIST = 128
LOG2E = math.log2(math.e)

ROW_TILE = 512
REC_ROWS = 512
REC_SUB = 16
REC_UNROLL = 2
ATTN_TILE = 256
ATTN_QTILES = 2
ATTN_GROUP = 2
BIAS_NEAR, BIAS_DIAG, BIAS_FAR, BIAS_MASKED = range(4)
VMEM_LIMIT = 56 * 1024 * 1024

F32 = jnp.float32
BF16 = jnp.bfloat16


def _rms(x, w):
    return x * lax.rsqrt(jnp.mean(x * x, axis=-1, keepdims=True) + EPS) * w


def _sigmoid(x):
    return 1.0 / (1.0 + jnp.exp(-x))


def _params(*sem):
    return pltpu.CompilerParams(dimension_semantics=sem, vmem_limit_bytes=VMEM_LIMIT)


def _resident(shape):
    return pl.BlockSpec(shape, lambda *_: (0,) * len(shape), pipeline_mode=pl.Buffered(1))


def _hgrn_in_kernel(h_ref, nw_ref, w_ref, loglb_ref, log1mlb_ref, onemlb_ref,
                    q_ref, b_ref, k_ref, v_ref, sg_ref):
    d = h_ref.shape[1]
    xn = _rms(h_ref[...], nw_ref[...]).astype(BF16)

    def proj(c):
        return jnp.dot(xn, w_ref[:, c * d:(c + 1) * d], preferred_element_type=F32)

    q = proj(0)
    q_ref[...] = (q * _sigmoid(q)).astype(BF16)
    f = proj(1)
    log_sig = jnp.minimum(f, 0.0) - jnp.log1p(jnp.exp(-jnp.abs(f)))
    a = loglb_ref[...]
    c = log1mlb_ref[...] + log_sig
    log_f = jnp.maximum(a, c) + jnp.log1p(jnp.exp(-jnp.abs(a - c)))
    b = log_f * LOG2E
    row_in_blk = lax.broadcasted_iota(jnp.int32, b.shape, 0) % REC_SUB
    shift = 1
    while shift < REC_SUB:
        b = b + jnp.where(row_in_blk >= shift, pltpu.roll(b, shift, axis=0), 0.0)
        shift *= 2
    b_ref[...] = b
    k_ref[...] = (onemlb_ref[...] / (1.0 + jnp.exp(f))).astype(BF16)
    v_ref[...] = proj(2).astype(BF16)
    g = proj(3)
    sg_ref[...] = (g * _sigmoid(g)).astype(BF16)


def _hgrn_in(h, nw, w, lb):
    t, d = h.shape
    row = lambda i: (i, 0)
    vec = lambda a: a.reshape(1, d).astype(F32)
    out_bf = jax.ShapeDtypeStruct((t, d), BF16)
    return pl.pallas_call(
        _hgrn_in_kernel,
        grid=(t // ROW_TILE,),
        in_specs=[pl.BlockSpec((ROW_TILE, d), row), _resident((1, d)), _resident(w.shape),
                  _resident((1, d)), _resident((1, d)), _resident((1, d))],
        out_specs=[pl.BlockSpec((ROW_TILE, d), row)] * 5,
        out_shape=[out_bf, jax.ShapeDtypeStruct((t, d), F32), out_bf, out_bf, out_bf],
        compiler_params=_params("parallel"),
        name="hgrn_in",
    )(h, vec(nw), w, vec(jnp.log(lb)), vec(jnp.log1p(-lb)), vec(1.0 - lb))


def _hgrn_rec_kernel(q_ref, b_ref, k_ref, v_ref, sg_ref, gnw_ref, o_ref, st_ref, *, nheads):
    sub = REC_SUB
    half = sub // 2

    @pl.when(pl.program_id(1) == 0)
    def _():
        st_ref[...] = jnp.zeros_like(st_ref)

    col = lax.broadcasted_iota(jnp.int32, (half, sub), 1)
    causal = (lax.broadcasted_iota(jnp.int32, (sub, sub), 1)
              <= lax.broadcasted_iota(jnp.int32, (sub, sub), 0))
    gnw = gnw_ref[...]
    heads = range(nheads)
    blocks = range(REC_UNROLL)
    cs = [slice(h * HEAD, (h + 1) * HEAD) for h in heads]
    nt = (((1,), (1,)), ((), ()))
    tn = (((0,), (0,)), ((), ()))

    def body(trip, carry):
        rows = [pl.ds(pl.multiple_of((trip * REC_UNROLL + j) * sub, sub), sub) for j in blocks]
        b = [[b_ref[rows[j], cs[h]] for h in heads] for j in blocks]
        q = [[q_ref[rows[j], cs[h]].astype(F32) for h in heads] for j in blocks]

        upd = [[None] * nheads for _ in blocks]
        for j in blocks:
            for h in heads:
                kd = k_ref[rows[j], cs[h]].astype(F32) * jnp.exp2(b[j][h][sub - 1:sub, :] - b[j][h])
                upd[j][h] = lax.dot_general(v_ref[rows[j], cs[h]], kd.astype(BF16), tn,
                                            preferred_element_type=F32)

        inter = [[None] * nheads for _ in blocks]
        for h in heads:
            st = st_ref[h]
            for j in blocks:
                qe = (q[j][h] * jnp.exp2(b[j][h])).astype(BF16)
                inter[j][h] = lax.dot_general(qe, st.astype(BF16), nt, preferred_element_type=F32)
                st = st * jnp.exp2(b[j][h][sub - 1:sub, :]) + upd[j][h]
            st_ref[h] = st

        for j in blocks:
            raw = [None] * nheads
            for h in heads:
                lo = slice(0, half)
                hi = slice(half, sub)
                pieces = []
                for s in range(sub):
                    b_s = b[j][h][s:s + 1, :]
                    if s < half:
                        pieces.append(q[j][h][lo] * jnp.exp2(jnp.minimum(b[j][h][lo] - b_s, 0.0)))
                    pieces.append(q[j][h][hi] * jnp.exp2(jnp.minimum(b[j][h][hi] - b_s, 0.0)))
                stacked = jnp.concatenate(pieces, axis=0).astype(BF16)
                raw[h] = lax.dot_general(stacked, k_ref[rows[j], cs[h]], nt, preferred_element_type=F32)

            for h in heads:
                s_lo = jnp.zeros((half, sub), F32)
                s_hi = jnp.zeros((half, sub), F32)
                off = 0
                for s in range(sub):
                    if s < half:
                        s_lo = jnp.where(col == s, raw[h][off:off + half], s_lo)
                        off += half
                    s_hi = jnp.where(col == s, raw[h][off:off + half], s_hi)
                    off += half
                scores = jnp.where(causal, jnp.concatenate([s_lo, s_hi], axis=0), 0.0).astype(BF16)
                o = inter[j][h] + jnp.dot(scores, v_ref[rows[j], cs[h]], preferred_element_type=F32)
                og = _rms(o, gnw) * sg_ref[rows[j], cs[h]].astype(F32)
                o_ref[rows[j], cs[h]] = og.astype(BF16)
        return carry

    lax.fori_loop(0, q_ref.shape[0] // (sub * REC_UNROLL), body, 0)


def _hgrn_rec(q, b, k, v, sg, gnw, batch):
    t, d = q.shape
    nheads = d // HEAD
    nt = t // batch // REC_ROWS
    row = lambda b, i: (b * nt + i, 0)
    blk = pl.BlockSpec((REC_ROWS, d), row)
    return pl.pallas_call(
        functools.partial(_hgrn_rec_kernel, nheads=nheads),
        grid=(batch, nt),
        in_specs=[blk, blk, blk, blk, blk, _resident((1, HEAD))],
        out_specs=blk,
        out_shape=jax.ShapeDtypeStruct((t, d), BF16),
        scratch_shapes=[pltpu.VMEM((nheads, HEAD, HEAD), F32)],
        compiler_params=_params("parallel", "arbitrary"),
        name="hgrn_rec",
    )(q, b, k, v, sg, gnw.reshape(1, HEAD).astype(F32))


def _out_proj_kernel(x_ref, w_ref, h_ref, nw_ref, o_ref):
    y = jnp.dot(x_ref[...], w_ref[...], preferred_element_type=F32)
    o_ref[...] = h_ref[...] + _rms(y, nw_ref[...])


def _out_proj(x, w, h, nw):
    t, d = h.shape
    row = lambda i: (i, 0)
    return pl.pallas_call(
        _out_proj_kernel,
        grid=(t // ROW_TILE,),
        in_specs=[pl.BlockSpec((ROW_TILE, d), row), _resident(w.shape),
                  pl.BlockSpec((ROW_TILE, d), row), _resident((1, d))],
        out_specs=pl.BlockSpec((ROW_TILE, d), row),
        out_shape=jax.ShapeDtypeStruct((t, d), F32),
        compiler_params=_params("parallel"),
        name="out_proj",
    )(x, w, h, nw.reshape(1, d).astype(F32))


def _mlp_kernel(h_ref, n1_ref, wu_ref, wd_ref, n2_ref, o_ref, u_ref):
    h = h_ref[...]
    d = h.shape[1]
    xn = _rms(h, n1_ref[...]).astype(BF16)
    for c in range(wu_ref.shape[1] // d):
        u = jnp.maximum(jnp.dot(xn, wu_ref[:, c * d:(c + 1) * d], preferred_element_type=F32), 0.0)
        u_ref[:, c * d:(c + 1) * d] = (u * u).astype(BF16)
    y = jnp.dot(u_ref[...], wd_ref[...], preferred_element_type=F32)
    o_ref[...] = h + _rms(y, n2_ref[...])


def _mlp(h, n1, wu, wd, n2):
    t, d = h.shape
    row = lambda i: (i, 0)
    vec = lambda a: a.reshape(1, d).astype(F32)
    return pl.pallas_call(
        _mlp_kernel,
        grid=(t // ROW_TILE,),
        in_specs=[pl.BlockSpec((ROW_TILE, d), row), _resident((1, d)), _resident(wu.shape),
                  _resident(wd.shape), _resident((1, d))],
        out_specs=pl.BlockSpec((ROW_TILE, d), row),
        out_shape=jax.ShapeDtypeStruct((t, d), F32),
        scratch_shapes=[pltpu.VMEM((ROW_TILE, wu.shape[1]), BF16)],
        compiler_params=_params("parallel"),
        name="mlp",
    )(h, vec(n1), wu, wd, vec(n2))


def _norm_mm_kernel(h_ref, nw_ref, w_ref, *o_refs, scale):
    d = h_ref.shape[1]
    xn = _rms(h_ref[...], nw_ref[...]).astype(BF16)
    for c, o_ref in enumerate(o_refs):
        y = jnp.dot(xn, w_ref[:, c * d:(c + 1) * d], preferred_element_type=F32)
        o_ref[...] = (y * scale).astype(BF16)


def _norm_mm(h, nw, w, scale=1.0):
    t, d = h.shape
    n_out = w.shape[1] // d
    row = lambda i: (i, 0)
    return pl.pallas_call(
        functools.partial(_norm_mm_kernel, scale=scale),
        grid=(t // ROW_TILE,),
        in_specs=[pl.BlockSpec((ROW_TILE, d), row), _resident((1, d)), _resident(w.shape)],
        out_specs=[pl.BlockSpec((ROW_TILE, d), row)] * n_out,
        out_shape=[jax.ShapeDtypeStruct((t, d), BF16)] * n_out,
        compiler_params=_params("parallel"),
        name="norm_mm",
    )(h, nw.reshape(1, d).astype(F32), w)


def _rel_bucket(rel):
    half = REL_BUCKETS // 2
    max_exact = half // 2
    ret = jnp.where(rel > 0, half, 0)
    n = jnp.abs(rel)
    nf = jnp.maximum(n, 1).astype(jnp.float32)
    large = max_exact + (jnp.log(nf / max_exact) / math.log(REL_MAX_DIST / max_exact)
                         * (half - max_exact)).astype(jnp.int32)
    large = jnp.minimum(large, half - 1)
    return ret + jnp.where(n < max_exact, n, large)


def _bias_tables(rel_bias, nheads):
    tile = ATTN_TILE
    kp = jnp.arange(tile)[:, None]
    qp = jnp.arange(tile)[None, :]
    table = rel_bias.astype(F32) * LOG2E

    def lookup(rel):
        onehot = (_rel_bucket(rel)[..., None] == jnp.arange(REL_BUCKETS)).astype(F32)
        b = jnp.dot(onehot, table, precision=lax.Precision.HIGHEST)
        return b.reshape(tile, tile, nheads, 2).transpose(2, 3, 0, 1)

    far = table[_rel_bucket(jnp.int32(-(tile + 1)))]
    kinds = [None] * 4
    kinds[BIAS_NEAR] = lookup(kp - tile - qp)
    kinds[BIAS_DIAG] = jnp.where(kp // CHUNK <= qp // CHUNK, lookup(kp - qp), NEG_INF)
    kinds[BIAS_FAR] = jnp.broadcast_to(far.reshape(nheads, 2, 1, 1), (nheads, 2, tile, tile))
    kinds[BIAS_MASKED] = jnp.full((nheads, 2, tile, tile), NEG_INF, F32)
    return jnp.stack(kinds, axis=2), far


def _attn_kernel(lam_ref, far_ref, q_ref, k_ref, vt_ref, bias_ref, sw_ref, o_ref,
                 m_ref, l_ref, alpha_ref, acc_ref, s_ref, p_ref, qm_ref, *, out_scale):
    h = pl.program_id(1)
    i0 = pl.program_id(2) * ATTN_QTILES
    dh = HEAD // 2
    group = ATTN_GROUP
    tile = ATTN_TILE
    width = group * tile
    nsteps = (i0 + ATTN_QTILES - 1) // group + 1

    q = q_ref[...]
    lane = lax.broadcasted_iota(jnp.int32, q.shape, 1)
    zero = jnp.zeros_like(q)
    qm_ref[0] = jnp.where(lane < dh, q, zero)
    qm_ref[1] = jnp.where(lane >= dh, q, zero)

    def logits(t):
        start = pl.multiple_of(t * width, width)
        kblk = k_ref[pl.ds(start, width), :]
        for m in range(2):
            s_ref[m] = lax.dot_general(kblk, qm_ref[m], (((1,), (1,)), ((), ())),
                                       preferred_element_type=F32)

    def tile_kind(j, i):
        return jnp.where(j <= i - 2, BIAS_FAR,
                         jnp.where(j == i - 1, BIAS_NEAR, jnp.where(j == i, BIAS_DIAG, BIAS_MASKED)))

    def softmax(t, edge):
        for m in range(2):
            s = s_ref[m]
            if edge:
                s = jnp.concatenate(
                    [jnp.concatenate(
                        [s[g * tile:(g + 1) * tile, u * tile:(u + 1) * tile]
                         + bias_ref[0, m, tile_kind(t * group + g, i0 + u)] for u in range(ATTN_QTILES)], axis=1)
                     for g in range(group)], axis=0)
                m_cur = jnp.max(s, axis=0, keepdims=True)
            else:
                far_c = far_ref[2 * h + m]
                m_cur = jnp.max(s, axis=0, keepdims=True) + far_c
            m_prev = m_ref[m]
            m_new = jnp.maximum(m_prev, m_cur)
            alpha = jnp.exp2(m_prev - m_new)
            shift = m_new if edge else m_new - far_c
            p = jnp.exp2(s - shift)
            l_ref[m] = alpha * l_ref[m] + jnp.sum(p, axis=0, keepdims=True)
            p_ref[m] = p.astype(BF16)
            alpha_ref[m] = alpha
            m_ref[m] = m_new

    def values(t):
        vtblk = vt_ref[0, jnp.maximum(t, 0)]
        for m in range(2):
            acc_ref[m] = alpha_ref[m] * acc_ref[m] + jnp.dot(vtblk, p_ref[m], preferred_element_type=F32)

    m_ref[...] = jnp.full_like(m_ref, NEG_INF)
    l_ref[...] = jnp.zeros_like(l_ref)
    alpha_ref[...] = jnp.ones_like(alpha_ref)
    acc_ref[...] = jnp.zeros_like(acc_ref)
    p_ref[...] = jnp.zeros_like(p_ref)
    logits(0)

    def far_body(t, carry):
        values(t - 1)
        softmax(t, edge=False)
        logits(t + 1)
        return carry

    lax.fori_loop(0, jnp.maximum(nsteps - 2, 0), far_body, 0)

    @pl.when(nsteps >= 2)
    def _():
        values(nsteps - 3)
        softmax(nsteps - 2, edge=True)
        logits(nsteps - 1)

    values(nsteps - 2)
    softmax(nsteps - 1, edge=True)
    values(nsteps - 1)

    o_t = acc_ref[0] / l_ref[0] - lam_ref[0] * (acc_ref[1] / l_ref[1])
    o_ref[...] = (_rms(o_t.T, sw_ref[...]) * out_scale).astype(BF16)


def _diff_attn(q, k, vt, bias_tiles, far, lam, subln_w, batch, out_scale):
    t, d = q.shape
    seq = t // batch
    nheads = d // HEAD
    tile = ATTN_TILE
    width = ATTN_GROUP * tile
    qrows = ATTN_QTILES * tile
    nq = seq // qrows
    smem = pl.BlockSpec(memory_space=pltpu.SMEM)
    q_spec = pl.BlockSpec((qrows, HEAD), lambda b, h, i: (b * nq + i, h))
    return pl.pallas_call(
        functools.partial(_attn_kernel, out_scale=out_scale),
        grid=(batch, nheads, nq),
        in_specs=[smem, smem, q_spec,
                  pl.BlockSpec((seq, HEAD), lambda b, h, i: (b, h)),
                  pl.BlockSpec((1, seq // width, HEAD, width), lambda b, h, i: (b * nheads + h, 0, 0, 0)),
                  pl.BlockSpec((1, 2, 4, tile, tile), lambda b, h, i: (h, 0, 0, 0, 0)),
                  pl.BlockSpec((1, HEAD), lambda b, h, i: (0, 0))],
        out_specs=q_spec,
        out_shape=jax.ShapeDtypeStruct((t, d), BF16),
        scratch_shapes=[pltpu.VMEM((2, 1, qrows), F32)] * 3
        + [pltpu.VMEM((2, HEAD, qrows), F32), pltpu.VMEM((2, width, qrows), F32),
           pltpu.VMEM((2, width, qrows), BF16), pltpu.VMEM((2, qrows, HEAD), BF16)],
        compiler_params=_params("parallel", "parallel", "arbitrary"),
        name="diff_attn",
    )(lam.reshape(1).astype(F32), far, q, k, vt, bias_tiles, subln_w.reshape(1, HEAD).astype(F32))


def kernel(x, a_norm_pre, a_norm_post, a_w_in, a_lb, a_gate_norm, a_w_out, kv_norm, w_kv,
           b_norm_pre, b_norm_post, b_w_q, b_lambda, b_subln, b_w_out, rel_bias,
           mlp_norm_pre, mlp_norm_post, mlp_w_up, mlp_w_down):
    batch, seq, d = x.shape
    n_a = a_w_in.shape[0]
    n_b = b_w_q.shape[0]
    nheads = d // HEAD
    bf = lambda w: w.astype(BF16)

    lb_all = jnp.cumsum(jax.nn.softmax(a_lb.astype(F32), axis=0), axis=0)
    lb_all = lb_all - lb_all[0:1]

    h = x.reshape(batch * seq, d)
    for a in range(n_a):
        q, b, k, v, sg = _hgrn_in(h, a_norm_pre[a], bf(a_w_in[a]), lb_all[a])
        o = _hgrn_rec(q, b, k, v, sg, a_gate_norm[a], batch)
        h = _out_proj(o, bf(a_w_out[a]), h, a_norm_post[a])
        h = _mlp(h, mlp_norm_pre[a], bf(mlp_w_up[a]), bf(mlp_w_down[a]), mlp_norm_post[a])

    k_sh, v_sh = _norm_mm(h, kv_norm, bf(w_kv))
    width = ATTN_GROUP * ATTN_TILE
    vt = v_sh.reshape(batch, seq // width, width, nheads, HEAD).transpose(0, 3, 1, 4, 2)
    vt = vt.reshape(batch * nheads, seq // width, HEAD, width)
    bias_tiles, far = _bias_tables(rel_bias, nheads)
    q_scale = (HEAD // 2) ** -0.5 * LOG2E
    for bi in range(n_b):
        layer = n_a + bi
        lam_init = 0.8 - 0.6 * math.exp(-0.3 * layer)
        lp = b_lambda[bi].astype(F32)
        lam = jnp.exp(jnp.sum(lp[0] * lp[1])) - jnp.exp(jnp.sum(lp[2] * lp[3])) + lam_init
        (q,) = _norm_mm(h, b_norm_pre[bi], bf(b_w_q[bi]), scale=q_scale)
        o = _diff_attn(q, k_sh, vt, bias_tiles, far, lam, b_subln[bi], batch, 1.0 - lam_init)
        h = _out_proj(o, bf(b_w_out[bi]), h, b_norm_post[bi])
        h = _mlp(h, mlp_norm_pre[layer], bf(mlp_w_up[layer]), bf(mlp_w_down[layer]), mlp_norm_post[layer])
    return h.reshape(batch, seq, d)
```

```python
import functools
import math

import jax
import jax.numpy as jnp
from jax import lax
from jax.experimental import pallas as pl
from jax.experimental.pallas import tpu as pltpu

EPS = 1e-6
NEG_INF = -1e30
CHUNK = 64
HEAD = 128
REL_BUCKETS = 32
REL_MAX_DIST = 128
LOG2E = math.log2(math.e)

ROW_TILE = 512
REC_ROWS = 512
REC_SUB = 16
REC_UNROLL = 4
ATTN_TILE = 256
ATTN_QTILES = 2
ATTN_GROUP = 2
BIAS_NEAR, BIAS_DIAG, BIAS_FAR, BIAS_MASKED = range(4)
VMEM_LIMIT = 56 * 1024 * 1024

F32 = jnp.float32
BF16 = jnp.bfloat16


def _rms(x, w):
    return x * lax.rsqrt(jnp.mean(x * x, axis=-1, keepdims=True) + EPS) * w


def _sigmoid(x):
    return 1.0 / (1.0 + jnp.exp(-x))


def _params(*sem):
    return pltpu.CompilerParams(dimension_semantics=sem, vmem_limit_bytes=VMEM_LIMIT)


def _resident(shape):
    return pl.BlockSpec(shape, lambda *_: (0,) * len(shape), pipeline_mode=pl.Buffered(1))


def _hgrn_in_kernel(h_ref, nw_ref, w_ref, loglb_ref, log1mlb_ref, onemlb_ref,
                    q_ref, b_ref, k_ref, v_ref, sg_ref):
    d = h_ref.shape[1]
    xn = _rms(h_ref[...], nw_ref[...]).astype(BF16)

    def proj(c):
        return jnp.dot(xn, w_ref[:, c * d:(c + 1) * d], preferred_element_type=F32)

    q = proj(0)
    q_ref[...] = (q * _sigmoid(q)).astype(BF16)
    f = proj(1)
    e = jnp.exp(-jnp.abs(f))
    inv = 1.0 / (1.0 + e)
    log_sig = jnp.minimum(f, 0.0) + jnp.log(inv)
    a = loglb_ref[...]
    c = log1mlb_ref[...] + log_sig
    log_f = jnp.maximum(a, c) + jnp.log(1.0 + jnp.exp(-jnp.abs(a - c)))
    b = log_f * LOG2E
    row_in_blk = lax.broadcasted_iota(jnp.int32, b.shape, 0) % REC_SUB
    shift = 1
    while shift < REC_SUB:
        b = b + jnp.where(row_in_blk >= shift, pltpu.roll(b, shift, axis=0), 0.0)
        shift *= 2
    b_ref[...] = b
    k_ref[...] = (onemlb_ref[...] * jnp.where(f > 0.0, e * inv, inv)).astype(BF16)
    v_ref[...] = proj(2).astype(BF16)
    g = proj(3)
    sg_ref[...] = (g * _sigmoid(g)).astype(BF16)


def _hgrn_in(h, nw, w, lb):
    t, d = h.shape
    row = lambda i: (i, 0)
    vec = lambda a: a.reshape(1, d).astype(F32)
    out_bf = jax.ShapeDtypeStruct((t, d), BF16)
    return pl.pallas_call(
        _hgrn_in_kernel,
        grid=(t // ROW_TILE,),
        in_specs=[pl.BlockSpec((ROW_TILE, d), row), _resident((1, d)), _resident(w.shape),
                  _resident((1, d)), _resident((1, d)), _resident((1, d))],
        out_specs=[pl.BlockSpec((ROW_TILE, d), row)] * 5,
        out_shape=[out_bf, jax.ShapeDtypeStruct((t, d), F32), out_bf, out_bf, out_bf],
        compiler_params=_params("parallel"),
        name="hgrn_in",
    )(h, vec(nw), w, vec(jnp.log(lb)), vec(jnp.log1p(-lb)), vec(1.0 - lb))


def _hgrn_rec_kernel(q_ref, b_ref, k_ref, v_ref, sg_ref, gnw_ref, o_ref, st_ref, *, nheads):
    sub = REC_SUB
    half = sub // 2
    nblk = REC_UNROLL
    chunk = nblk * sub

    @pl.when(pl.program_id(1) == 0)
    def _():
        st_ref[...] = jnp.zeros_like(st_ref)

    col = lax.broadcasted_iota(jnp.int32, (half, sub), 1)
    causal = (lax.broadcasted_iota(jnp.int32, (sub, sub), 1)
              <= lax.broadcasted_iota(jnp.int32, (sub, sub), 0))
    gnw = gnw_ref[...]
    heads = range(nheads)
    blocks = range(nblk)
    cs = [slice(h * HEAD, (h + 1) * HEAD) for h in heads]
    nt = (((1,), (1,)), ((), ()))
    tn = (((0,), (0,)), ((), ()))

    def body(trip, carry):
        base = pl.multiple_of(trip * chunk, chunk)
        rows_all = pl.ds(base, chunk)
        rows = [pl.ds(pl.multiple_of(base + j * sub, sub), sub) for j in blocks]
        b = [[b_ref[rows[j], cs[h]] for h in heads] for j in blocks]
        q = [[q_ref[rows[j], cs[h]].astype(F32) for h in heads] for j in blocks]

        inter = [None] * nheads
        cross = [[None] * nheads for _ in blocks]
        for h in heads:
            tot = [b[j][h][sub - 1:sub, :] for j in blocks]
            start = [None] * (nblk + 1)
            for j in range(1, nblk + 1):
                start[j] = tot[j - 1] if start[j - 1] is None else start[j - 1] + tot[j - 1]
            qe = [q[j][h] * jnp.exp2(b[j][h]) for j in blocks]
            kd = [k_ref[rows[j], cs[h]].astype(F32) * jnp.exp2(tot[j] - b[j][h]) for j in blocks]
            q_chunk = [qe[j] if start[j] is None else qe[j] * jnp.exp2(start[j]) for j in blocks]
            k_chunk = [kd[j] if j == nblk - 1 else kd[j] * jnp.exp2(start[nblk] - start[j + 1]) for j in blocks]
            inter[h] = lax.dot_general(jnp.concatenate(q_chunk, axis=0).astype(BF16), st_ref[h].astype(BF16),
                                       nt, preferred_element_type=F32)
            upd = lax.dot_general(v_ref[rows_all, cs[h]], jnp.concatenate(k_chunk, axis=0).astype(BF16),
                                  tn, preferred_element_type=F32)
            st_ref[h] = st_ref[h] * jnp.exp2(start[nblk]) + upd
            for i in range(1, nblk):
                keys = [kd[j] if j == i - 1 else kd[j] * jnp.exp2(start[i] - start[j + 1]) for j in range(i)]
                cross[i][h] = lax.dot_general(qe[i].astype(BF16), jnp.concatenate(keys, axis=0).astype(BF16),
                                              nt, preferred_element_type=F32)

        for j in blocks:
            raw = [None] * nheads
            for h in heads:
                lo = slice(0, half)
                hi = slice(half, sub)
                pieces = []
                for s in range(sub):
                    b_s = b[j][h][s:s + 1, :]
                    if s < half:
                        pieces.append(q[j][h][lo] * jnp.exp2(jnp.minimum(b[j][h][lo] - b_s, 0.0)))
                    pieces.append(q[j][h][hi] * jnp.exp2(jnp.minimum(b[j][h][hi] - b_s, 0.0)))
                stacked = jnp.concatenate(pieces, axis=0).astype(BF16)
                raw[h] = lax.dot_general(stacked, k_ref[rows[j], cs[h]], nt, preferred_element_type=F32)

            for h in heads:
                s_lo = jnp.zeros((half, sub), F32)
                s_hi = jnp.zeros((half, sub), F32)
                off = 0
                for s in range(sub):
                    if s < half:
                        s_lo = jnp.where(col == s, raw[h][off:off + half], s_lo)
                        off += half
                    s_hi = jnp.where(col == s, raw[h][off:off + half], s_hi)
                    off += half
                scores = jnp.where(causal, jnp.concatenate([s_lo, s_hi], axis=0), 0.0).astype(BF16)
                o = inter[h][j * sub:(j + 1) * sub] + jnp.dot(scores, v_ref[rows[j], cs[h]],
                                                              preferred_element_type=F32)
                if j > 0:
                    o = o + jnp.dot(cross[j][h].astype(BF16), v_ref[pl.ds(base, j * sub), cs[h]],
                                    preferred_element_type=F32)
                og = _rms(o, gnw) * sg_ref[rows[j], cs[h]].astype(F32)
                o_ref[rows[j], cs[h]] = og.astype(BF16)
        return carry

    lax.fori_loop(0, q_ref.shape[0] // chunk, body, 0)


def _hgrn_rec(q, b, k, v, sg, gnw, batch):
    t, d = q.shape
    nheads = d // HEAD
    nt = t // batch // REC_ROWS
    row = lambda b, i: (b * nt + i, 0)
    blk = pl.BlockSpec((REC_ROWS, d), row)
    return pl.pallas_call(
        functools.partial(_hgrn_rec_kernel, nheads=nheads),
        grid=(batch, nt),
        in_specs=[blk, blk, blk, blk, blk, _resident((1, HEAD))],
        out_specs=blk,
        out_shape=jax.ShapeDtypeStruct((t, d), BF16),
        scratch_shapes=[pltpu.VMEM((nheads, HEAD, HEAD), F32)],
        compiler_params=_params("parallel", "arbitrary"),
        name="hgrn_rec",
    )(q, b, k, v, sg, gnw.reshape(1, HEAD).astype(F32))


def _proj_mlp_kernel(x_ref, wo_ref, h_ref, npost_ref, n1_ref, wu_ref, wd_ref, n2_ref, o_ref, u_ref):
    mix = jnp.dot(x_ref[...], wo_ref[...], preferred_element_type=F32)
    h = h_ref[...] + _rms(mix, npost_ref[...])
    d = h.shape[1]
    xn = _rms(h, n1_ref[...]).astype(BF16)
    for c in range(wu_ref.shape[1] // d):
        u = jnp.maximum(jnp.dot(xn, wu_ref[:, c * d:(c + 1) * d], preferred_element_type=F32), 0.0)
        u_ref[:, c * d:(c + 1) * d] = (u * u).astype(BF16)
    y = jnp.dot(u_ref[...], wd_ref[...], preferred_element_type=F32)
    o_ref[...] = h + _rms(y, n2_ref[...])


def _proj_mlp(x, wo, h, npost, n1, wu, wd, n2):
    t, d = h.shape
    row = lambda i: (i, 0)
    vec = lambda a: a.reshape(1, d).astype(F32)
    blk = pl.BlockSpec((ROW_TILE, d), row)
    return pl.pallas_call(
        _proj_mlp_kernel,
        grid=(t // ROW_TILE,),
        in_specs=[blk, _resident(wo.shape), blk, _resident((1, d)), _resident((1, d)), _resident(wu.shape),
                  _resident(wd.shape), _resident((1, d))],
        out_specs=blk,
        out_shape=jax.ShapeDtypeStruct((t, d), F32),
        scratch_shapes=[pltpu.VMEM((ROW_TILE, wu.shape[1]), BF16)],
        compiler_params=_params("parallel"),
        name="proj_mlp",
    )(x, wo, h, vec(npost), vec(n1), wu, wd, vec(n2))


def _norm_mm_kernel(h_ref, nw_ref, w_ref, *o_refs, scale):
    d = h_ref.shape[1]
    xn = _rms(h_ref[...], nw_ref[...]).astype(BF16)
    for c, o_ref in enumerate(o_refs):
        y = jnp.dot(xn, w_ref[:, c * d:(c + 1) * d], preferred_element_type=F32)
        o_ref[...] = (y * scale).astype(BF16)


def _norm_mm(h, nw, w, scale=1.0):
    t, d = h.shape
    n_out = w.shape[1] // d
    row = lambda i: (i, 0)
    return pl.pallas_call(
        functools.partial(_norm_mm_kernel, scale=scale),
        grid=(t // ROW_TILE,),
        in_specs=[pl.BlockSpec((ROW_TILE, d), row), _resident((1, d)), _resident(w.shape)],
        out_specs=[pl.BlockSpec((ROW_TILE, d), row)] * n_out,
        out_shape=[jax.ShapeDtypeStruct((t, d), BF16)] * n_out,
        compiler_params=_params("parallel"),
        name="norm_mm",
    )(h, nw.reshape(1, d).astype(F32), w)


def _rel_bucket(rel):
    half = REL_BUCKETS // 2
    max_exact = half // 2
    ret = jnp.where(rel > 0, half, 0)
    n = jnp.abs(rel)
    nf = jnp.maximum(n, 1).astype(jnp.float32)
    large = max_exact + (jnp.log(nf / max_exact) / math.log(REL_MAX_DIST / max_exact)
                         * (half - max_exact)).astype(jnp.int32)
    large = jnp.minimum(large, half - 1)
    return ret + jnp.where(n < max_exact, n, large)


def _bias_tables(rel_bias, nheads):
    tile = ATTN_TILE
    kp = jnp.arange(tile)[:, None]
    qp = jnp.arange(tile)[None, :]
    table = rel_bias.astype(F32) * LOG2E

    def lookup(rel):
        onehot = (_rel_bucket(rel)[..., None] == jnp.arange(REL_BUCKETS)).astype(F32)
        b = jnp.dot(onehot, table, precision=lax.Precision.HIGHEST)
        return b.reshape(tile, tile, nheads, 2).transpose(2, 3, 0, 1)

    far = table[_rel_bucket(jnp.int32(-(tile + 1)))]
    kinds = [None] * 4
    kinds[BIAS_NEAR] = lookup(kp - tile - qp)
    kinds[BIAS_DIAG] = jnp.where(kp // CHUNK <= qp // CHUNK, lookup(kp - qp), NEG_INF)
    kinds[BIAS_FAR] = jnp.broadcast_to(far.reshape(nheads, 2, 1, 1), (nheads, 2, tile, tile))
    kinds[BIAS_MASKED] = jnp.full((nheads, 2, tile, tile), NEG_INF, F32)
    return jnp.stack(kinds, axis=2), far


def _attn_kernel(lam_ref, far_ref, q_ref, k_ref, vt_ref, bias_ref, sw_ref, o_ref,
                 m_ref, l_ref, alpha_ref, acc_ref, s_ref, p_ref, qm_ref, *, out_scale):
    h = pl.program_id(1)
    i0 = pl.program_id(2) * ATTN_QTILES
    dh = HEAD // 2
    group = ATTN_GROUP
    tile = ATTN_TILE
    width = group * tile
    nsteps = (i0 + ATTN_QTILES - 1) // group + 1

    q = q_ref[...]
    lane = lax.broadcasted_iota(jnp.int32, q.shape, 1)
    zero = jnp.zeros_like(q)
    qm_ref[0] = jnp.where(lane < dh, q, zero)
    qm_ref[1] = jnp.where(lane >= dh, q, zero)

    def logits(t):
        start = pl.multiple_of(t * width, width)
        kblk = k_ref[pl.ds(start, width), :]
        for m in range(2):
            s_ref[m] = lax.dot_general(kblk, qm_ref[m], (((1,), (1,)), ((), ())),
                                       preferred_element_type=F32)

    def tile_kind(j, i):
        return jnp.where(j <= i - 2, BIAS_FAR,
                         jnp.where(j == i - 1, BIAS_NEAR, jnp.where(j == i, BIAS_DIAG, BIAS_MASKED)))

    def softmax(t, edge):
        for m in range(2):
            s = s_ref[m]
            if edge:
                s = jnp.concatenate(
                    [jnp.concatenate(
                        [s[g * tile:(g + 1) * tile, u * tile:(u + 1) * tile]
                         + bias_ref[0, m, tile_kind(t * group + g, i0 + u)] for u in range(ATTN_QTILES)], axis=1)
                     for g in range(group)], axis=0)
                m_cur = jnp.max(s, axis=0, keepdims=True)
            else:
                far_c = far_ref[2 * h + m]
                m_cur = jnp.max(s, axis=0, keepdims=True) + far_c
            m_prev = m_ref[m]
            m_new = jnp.maximum(m_prev, m_cur)
            alpha = jnp.exp2(m_prev - m_new)
            shift = m_new if edge else m_new - far_c
            p = jnp.exp2(s - shift)
            l_ref[m] = alpha * l_ref[m] + jnp.sum(p, axis=0, keepdims=True)
            p_ref[m] = p.astype(BF16)
            alpha_ref[m] = alpha
            m_ref[m] = m_new

    def values(t):
        vtblk = vt_ref[0, jnp.maximum(t, 0)]
        for m in range(2):
            acc_ref[m] = alpha_ref[m] * acc_ref[m] + jnp.dot(vtblk, p_ref[m], preferred_element_type=F32)

    m_ref[...] = jnp.full_like(m_ref, NEG_INF)
    l_ref[...] = jnp.zeros_like(l_ref)
    alpha_ref[...] = jnp.ones_like(alpha_ref)
    acc_ref[...] = jnp.zeros_like(acc_ref)
    p_ref[...] = jnp.zeros_like(p_ref)
    logits(0)

    def far_body(t, carry):
        values(t - 1)
        softmax(t, edge=False)
        logits(t + 1)
        return carry

    lax.fori_loop(0, jnp.maximum(nsteps - 2, 0), far_body, 0)

    @pl.when(nsteps >= 2)
    def _():
        values(nsteps - 3)
        softmax(nsteps - 2, edge=True)
        logits(nsteps - 1)

    values(nsteps - 2)
    softmax(nsteps - 1, edge=True)
    values(nsteps - 1)

    o_t = acc_ref[0] / l_ref[0] - lam_ref[0] * (acc_ref[1] / l_ref[1])
    o_ref[...] = (_rms(o_t.T, sw_ref[...]) * out_scale).astype(BF16)


def _diff_attn(q, k, vt, bias_tiles, far, lam, subln_w, batch, out_scale):
    t, d = q.shape
    seq = t // batch
    nheads = d // HEAD
    tile = ATTN_TILE
    width = ATTN_GROUP * tile
    qrows = ATTN_QTILES * tile
    nq = seq // qrows
    smem = pl.BlockSpec(memory_space=pltpu.SMEM)
    q_spec = pl.BlockSpec((qrows, HEAD), lambda b, h, i: (b * nq + i, h))
    return pl.pallas_call(
        functools.partial(_attn_kernel, out_scale=out_scale),
        grid=(batch, nheads, nq),
        in_specs=[smem, smem, q_spec,
                  pl.BlockSpec((seq, HEAD), lambda b, h, i: (b, h)),
                  pl.BlockSpec((1, seq // width, HEAD, width), lambda b, h, i: (b * nheads + h, 0, 0, 0)),
                  pl.BlockSpec((1, 2, 4, tile, tile), lambda b, h, i: (h, 0, 0, 0, 0)),
                  pl.BlockSpec((1, HEAD), lambda b, h, i: (0, 0))],
        out_specs=q_spec,
        out_shape=jax.ShapeDtypeStruct((t, d), BF16),
        scratch_shapes=[pltpu.VMEM((2, 1, qrows), F32)] * 3
        + [pltpu.VMEM((2, HEAD, qrows), F32), pltpu.VMEM((2, width, qrows), F32),
           pltpu.VMEM((2, width, qrows), BF16), pltpu.VMEM((2, qrows, HEAD), BF16)],
        compiler_params=_params("parallel", "parallel", "arbitrary"),
        name="diff_attn",
    )(lam.reshape(1).astype(F32), far, q, k, vt, bias_tiles, subln_w.reshape(1, HEAD).astype(F32))


def kernel(x, a_norm_pre, a_norm_post, a_w_in, a_lb, a_gate_norm, a_w_out, kv_norm, w_kv,
           b_norm_pre, b_norm_post, b_w_q, b_lambda, b_subln, b_w_out, rel_bias,
           mlp_norm_pre, mlp_norm_post, mlp_w_up, mlp_w_down):
    batch, seq, d = x.shape
    n_a = a_w_in.shape[0]
    n_b = b_w_q.shape[0]
    nheads = d // HEAD
    bf = lambda w: w.astype(BF16)

    lb_all = jnp.cumsum(jax.nn.softmax(a_lb.astype(F32), axis=0), axis=0)
    lb_all = lb_all - lb_all[0:1]

    h = x.reshape(batch * seq, d)
    for a in range(n_a):
        q, b, k, v, sg = _hgrn_in(h, a_norm_pre[a], bf(a_w_in[a]), lb_all[a])
        o = _hgrn_rec(q, b, k, v, sg, a_gate_norm[a], batch)
        h = _proj_mlp(o, bf(a_w_out[a]), h, a_norm_post[a],
                      mlp_norm_pre[a], bf(mlp_w_up[a]), bf(mlp_w_down[a]), mlp_norm_post[a])

    k_sh, v_sh = _norm_mm(h, kv_norm, bf(w_kv))
    width = ATTN_GROUP * ATTN_TILE
    vt = v_sh.reshape(batch, seq // width, width, nheads, HEAD).transpose(0, 3, 1, 4, 2)
    vt = vt.reshape(batch * nheads, seq // width, HEAD, width)
    bias_tiles, far = _bias_tables(rel_bias, nheads)
    q_scale = (HEAD // 2) ** -0.5 * LOG2E
    for bi in range(n_b):
        layer = n_a + bi
        lam_init = 0.8 - 0.6 * math.exp(-0.3 * layer)
        lp = b_lambda[bi].astype(F32)
        lam = jnp.exp(jnp.sum(lp[0] * lp[1])) - jnp.exp(jnp.sum(lp[2] * lp[3])) + lam_init
        (q,) = _norm_mm(h, b_norm_pre[bi], bf(b_w_q[bi]), scale=q_scale)
        o = _diff_attn(q, k_sh, vt, bias_tiles, far, lam, b_subln[bi], batch, 1.0 - lam_init)
        h = _proj_mlp(o, bf(b_w_out[bi]), h, b_norm_post[bi],
                      mlp_norm_pre[layer], bf(mlp_w_up[layer]), bf(mlp_w_down[layer]), mlp_norm_post[layer])
    return h.reshape(batch, seq, d)
```

```python
import functools
import math

import jax
import jax.numpy as jnp
from jax import lax
from jax.experimental import pallas as pl
from jax.experimental.pallas import tpu as pltpu

EPS = 1e-6
NEG_INF = -1e30
CHUNK = 64
HEAD = 128
REL_BUCKETS = 32
REL_MAX_DIST = 128
LOG2E = math.log2(math.e)

ROW_TILE = 512
REC_ROWS = 512
REC_SUB = 16
REC_UNROLL = 4
ATTN_TILE = 256
ATTN_QTILES = 2
ATTN_GROUP = 2
BIAS_NEAR, BIAS_DIAG, BIAS_FAR, BIAS_MASKED = range(4)
VMEM_LIMIT = 56 * 1024 * 1024

F32 = jnp.float32
BF16 = jnp.bfloat16


def _rms(x, w):
    return x * lax.rsqrt(jnp.mean(x * x, axis=-1, keepdims=True) + EPS) * w


def _sigmoid(x):
    return 1.0 / (1.0 + jnp.exp(-x))


def _params(*sem):
    return pltpu.CompilerParams(dimension_semantics=sem, vmem_limit_bytes=VMEM_LIMIT)


def _resident(shape):
    return pl.BlockSpec(shape, lambda *_: (0,) * len(shape), pipeline_mode=pl.Buffered(1))


def _hgrn_in_kernel(h_ref, nw_ref, w_ref, loglb_ref, log1mlb_ref, onemlb_ref,
                    q_ref, b_ref, k_ref, v_ref, sg_ref):
    d = h_ref.shape[1]
    xn = _rms(h_ref[...], nw_ref[...]).astype(BF16)

    def proj(c):
        return jnp.dot(xn, w_ref[:, c * d:(c + 1) * d], preferred_element_type=F32)

    q = proj(0)
    q_ref[...] = (q * _sigmoid(q)).astype(BF16)
    f = proj(1)
    e = jnp.exp(-jnp.abs(f))
    inv = 1.0 / (1.0 + e)
    log_sig = jnp.minimum(f, 0.0) + jnp.log(inv)
    a = loglb_ref[...]
    c = log1mlb_ref[...] + log_sig
    log_f = jnp.maximum(a, c) + jnp.log(1.0 + jnp.exp(-jnp.abs(a - c)))
    b = log_f * LOG2E
    row_in_blk = lax.broadcasted_iota(jnp.int32, b.shape, 0) % REC_SUB
    shift = 1
    while shift < REC_SUB:
        b = b + jnp.where(row_in_blk >= shift, pltpu.roll(b, shift, axis=0), 0.0)
        shift *= 2
    b_ref[...] = b
    k_ref[...] = (onemlb_ref[...] * jnp.where(f > 0.0, e * inv, inv)).astype(BF16)
    v_ref[...] = proj(2).astype(BF16)
    g = proj(3)
    sg_ref[...] = (g * _sigmoid(g)).astype(BF16)


def _hgrn_in(h, nw, w, lb):
    t, d = h.shape
    row = lambda i: (i, 0)
    vec = lambda a: a.reshape(1, d).astype(F32)
    out_bf = jax.ShapeDtypeStruct((t, d), BF16)
    return pl.pallas_call(
        _hgrn_in_kernel,
        grid=(t // ROW_TILE,),
        in_specs=[pl.BlockSpec((ROW_TILE, d), row), _resident((1, d)), _resident(w.shape),
                  _resident((1, d)), _resident((1, d)), _resident((1, d))],
        out_specs=[pl.BlockSpec((ROW_TILE, d), row)] * 5,
        out_shape=[out_bf, jax.ShapeDtypeStruct((t, d), F32), out_bf, out_bf, out_bf],
        compiler_params=_params("parallel"),
        name="hgrn_in",
    )(h, vec(nw), w, vec(jnp.log(lb)), vec(jnp.log1p(-lb)), vec(1.0 - lb))


def _hgrn_rec_kernel(q_ref, b_ref, k_ref, v_ref, sg_ref, gnw_ref, o_ref, st_ref, *, nheads):
    sub = REC_SUB
    half = sub // 2
    nblk = REC_UNROLL
    chunk = nblk * sub

    @pl.when(pl.program_id(1) == 0)
    def _():
        st_ref[...] = jnp.zeros_like(st_ref)

    col = lax.broadcasted_iota(jnp.int32, (half, sub), 1)
    causal = (lax.broadcasted_iota(jnp.int32, (sub, sub), 1)
              <= lax.broadcasted_iota(jnp.int32, (sub, sub), 0))
    gnw = gnw_ref[...]
    heads = range(nheads)
    blocks = range(nblk)
    cs = [slice(h * HEAD, (h + 1) * HEAD) for h in heads]
    nt = (((1,), (1,)), ((), ()))
    tn = (((0,), (0,)), ((), ()))

    def body(trip, carry):
        base = pl.multiple_of(trip * chunk, chunk)
        rows_all = pl.ds(base, chunk)
        rows = [pl.ds(pl.multiple_of(base + j * sub, sub), sub) for j in blocks]
        b = [[b_ref[rows[j], cs[h]] for h in heads] for j in blocks]
        q = [[q_ref[rows[j], cs[h]].astype(F32) for h in heads] for j in blocks]

        inter = [None] * nheads
        cross = [[None] * nheads for _ in blocks]
        for h in heads:
            tot = [b[j][h][sub - 1:sub, :] for j in blocks]
            start = [None] * (nblk + 1)
            for j in range(1, nblk + 1):
                start[j] = tot[j - 1] if start[j - 1] is None else start[j - 1] + tot[j - 1]
            qe = [q[j][h] * jnp.exp2(b[j][h]) for j in blocks]
            kd = [k_ref[rows[j], cs[h]].astype(F32) * jnp.exp2(tot[j] - b[j][h]) for j in blocks]
            q_chunk = [qe[j] if start[j] is None else qe[j] * jnp.exp2(start[j]) for j in blocks]
            k_chunk = [kd[j] if j == nblk - 1 else kd[j] * jnp.exp2(start[nblk] - start[j + 1]) for j in blocks]
            inter[h] = lax.dot_general(jnp.concatenate(q_chunk, axis=0).astype(BF16), st_ref[h].astype(BF16),
                                       nt, preferred_element_type=F32)
            upd = lax.dot_general(v_ref[rows_all, cs[h]], jnp.concatenate(k_chunk, axis=0).astype(BF16),
                                  tn, preferred_element_type=F32)
            st_ref[h] = st_ref[h] * jnp.exp2(start[nblk]) + upd
            for i in range(1, nblk):
                keys = [kd[j] if j == i - 1 else kd[j] * jnp.exp2(start[i] - start[j + 1]) for j in range(i)]
                cross[i][h] = lax.dot_general(qe[i].astype(BF16), jnp.concatenate(keys, axis=0).astype(BF16),
                                              nt, preferred_element_type=F32)

        for j in blocks:
            raw = [None] * nheads
            for h in heads:
                lo = slice(0, half)
                hi = slice(half, sub)
                pieces = []
                for s in range(sub):
                    b_s = b[j][h][s:s + 1, :]
                    if s < half:
                        pieces.append(q[j][h][lo] * jnp.exp2(jnp.minimum(b[j][h][lo] - b_s, 0.0)))
                    pieces.append(q[j][h][hi] * jnp.exp2(jnp.minimum(b[j][h][hi] - b_s, 0.0)))
                stacked = jnp.concatenate(pieces, axis=0).astype(BF16)
                raw[h] = lax.dot_general(stacked, k_ref[rows[j], cs[h]], nt, preferred_element_type=F32)

            for h in heads:
                s_lo = jnp.zeros((half, sub), F32)
                s_hi = jnp.zeros((half, sub), F32)
                off = 0
                for s in range(sub):
                    if s < half:
                        s_lo = jnp.where(col == s, raw[h][off:off + half], s_lo)
                        off += half
                    s_hi = jnp.where(col == s, raw[h][off:off + half], s_hi)
                    off += half
                scores = jnp.where(causal, jnp.concatenate([s_lo, s_hi], axis=0), 0.0).astype(BF16)
                o = inter[h][j * sub:(j + 1) * sub] + jnp.dot(scores, v_ref[rows[j], cs[h]],
                                                              preferred_element_type=F32)
                if j > 0:
                    o = o + jnp.dot(cross[j][h].astype(BF16), v_ref[pl.ds(base, j * sub), cs[h]],
                                    preferred_element_type=F32)
                og = _rms(o, gnw) * sg_ref[rows[j], cs[h]].astype(F32)
                o_ref[rows[j], cs[h]] = og.astype(BF16)
        return carry

    lax.fori_loop(0, q_ref.shape[0] // chunk, body, 0)


def _hgrn_rec(q, b, k, v, sg, gnw, batch):
    t, d = q.shape
    nheads = d // HEAD
    nt = t // batch // REC_ROWS
    row = lambda b, i: (b * nt + i, 0)
    blk = pl.BlockSpec((REC_ROWS, d), row)
    return pl.pallas_call(
        functools.partial(_hgrn_rec_kernel, nheads=nheads),
        grid=(batch, nt),
        in_specs=[blk, blk, blk, blk, blk, _resident((1, HEAD))],
        out_specs=blk,
        out_shape=jax.ShapeDtypeStruct((t, d), BF16),
        scratch_shapes=[pltpu.VMEM((nheads, HEAD, HEAD), F32)],
        compiler_params=_params("parallel", "arbitrary"),
        name="hgrn_rec",
    )(q, b, k, v, sg, gnw.reshape(1, HEAD).astype(F32))


def _proj_mlp_kernel(x_ref, wo_ref, h_ref, npost_ref, n1_ref, wu_ref, wd_ref, n2_ref, o_ref, u_ref):
    mix = jnp.dot(x_ref[...], wo_ref[...], preferred_element_type=F32)
    h = h_ref[...] + _rms(mix, npost_ref[...])
    d = h.shape[1]
    xn = _rms(h, n1_ref[...]).astype(BF16)
    for c in range(wu_ref.shape[1] // d):
        u = jnp.maximum(jnp.dot(xn, wu_ref[:, c * d:(c + 1) * d], preferred_element_type=F32), 0.0)
        u_ref[:, c * d:(c + 1) * d] = (u * u).astype(BF16)
    y = jnp.dot(u_ref[...], wd_ref[...], preferred_element_type=F32)
    o_ref[...] = h + _rms(y, n2_ref[...])


def _proj_mlp(x, wo, h, npost, n1, wu, wd, n2):
    t, d = h.shape
    row = lambda i: (i, 0)
    vec = lambda a: a.reshape(1, d).astype(F32)
    blk = pl.BlockSpec((ROW_TILE, d), row)
    return pl.pallas_call(
        _proj_mlp_kernel,
        grid=(t // ROW_TILE,),
        in_specs=[blk, _resident(wo.shape), blk, _resident((1, d)), _resident((1, d)), _resident(wu.shape),
                  _resident(wd.shape), _resident((1, d))],
        out_specs=blk,
        out_shape=jax.ShapeDtypeStruct((t, d), F32),
        scratch_shapes=[pltpu.VMEM((ROW_TILE, wu.shape[1]), BF16)],
        compiler_params=_params("parallel"),
        name="proj_mlp",
    )(x, wo, h, vec(npost), vec(n1), wu, wd, vec(n2))


def _norm_mm_kernel(h_ref, nw_ref, w_ref, *o_refs, scale, split_maps):
    d = h_ref.shape[1]
    xn = _rms(h_ref[...], nw_ref[...]).astype(BF16)
    if split_maps:
        y = (jnp.dot(xn, w_ref[...], preferred_element_type=F32) * scale).astype(BF16)
        first = lax.broadcasted_iota(jnp.int32, y.shape, 1) % HEAD < HEAD // 2
        zero = jnp.zeros_like(y)
        o_refs[0][...] = jnp.where(first, y, zero)
        o_refs[1][...] = jnp.where(first, zero, y)
        return
    for c, o_ref in enumerate(o_refs):
        y = jnp.dot(xn, w_ref[:, c * d:(c + 1) * d], preferred_element_type=F32)
        o_ref[...] = (y * scale).astype(BF16)


def _norm_mm(h, nw, w, scale=1.0, split_maps=False):
    t, d = h.shape
    n_out = 2 if split_maps else w.shape[1] // d
    row = lambda i: (i, 0)
    return pl.pallas_call(
        functools.partial(_norm_mm_kernel, scale=scale, split_maps=split_maps),
        grid=(t // ROW_TILE,),
        in_specs=[pl.BlockSpec((ROW_TILE, d), row), _resident((1, d)), _resident(w.shape)],
        out_specs=[pl.BlockSpec((ROW_TILE, d), row)] * n_out,
        out_shape=[jax.ShapeDtypeStruct((t, d), BF16)] * n_out,
        compiler_params=_params("parallel"),
        name="norm_mm",
    )(h, nw.reshape(1, d).astype(F32), w)


def _rel_bucket(rel):
    half = REL_BUCKETS // 2
    max_exact = half // 2
    ret = jnp.where(rel > 0, half, 0)
    n = jnp.abs(rel)
    nf = jnp.maximum(n, 1).astype(jnp.float32)
    large = max_exact + (jnp.log(nf / max_exact) / math.log(REL_MAX_DIST / max_exact)
                         * (half - max_exact)).astype(jnp.int32)
    large = jnp.minimum(large, half - 1)
    return ret + jnp.where(n < max_exact, n, large)


def _bias_tables(rel_bias, nheads):
    tile = ATTN_TILE
    kp = jnp.arange(tile)[:, None]
    qp = jnp.arange(tile)[None, :]
    table = rel_bias.astype(F32) * LOG2E

    def lookup(rel):
        onehot = (_rel_bucket(rel)[..., None] == jnp.arange(REL_BUCKETS)).astype(F32)
        b = jnp.dot(onehot, table, precision=lax.Precision.HIGHEST)
        return b.reshape(tile, tile, nheads, 2).transpose(2, 3, 0, 1)

    far = table[_rel_bucket(jnp.int32(-(tile + 1)))]
    kinds = [None] * 4
    kinds[BIAS_NEAR] = lookup(kp - tile - qp)
    kinds[BIAS_DIAG] = jnp.where(kp // CHUNK <= qp // CHUNK, lookup(kp - qp), NEG_INF)
    kinds[BIAS_FAR] = jnp.broadcast_to(far.reshape(nheads, 2, 1, 1), (nheads, 2, tile, tile))
    kinds[BIAS_MASKED] = jnp.full((nheads, 2, tile, tile), NEG_INF, F32)
    return jnp.stack(kinds, axis=2), far


def _attn_kernel(lam_ref, far_ref, qa_ref, qb_ref, k_ref, vt_ref, bias_ref, sw_ref, o_ref,
                 m_ref, l_ref, alpha_ref, acc_ref, s_ref, p_ref, *, out_scale):
    h = pl.program_id(1)
    qi = pl.program_id(2)
    i0 = qi * ATTN_QTILES
    group = ATTN_GROUP
    tile = ATTN_TILE
    width = group * tile
    qrows = ATTN_QTILES * tile
    nsteps = (i0 + ATTN_QTILES - 1) // group + 1

    def logits(t, q_index):
        start = pl.multiple_of(t * width, width)
        kblk = k_ref[pl.ds(start, width), :]
        q_start = pl.multiple_of(q_index * qrows, qrows)
        for m, q_ref in enumerate((qa_ref, qb_ref)):
            s_ref[m] = lax.dot_general(kblk, q_ref[pl.ds(q_start, qrows), :], (((1,), (1,)), ((), ())),
                                       preferred_element_type=F32)

    def tile_kind(j, i):
        return jnp.where(j <= i - 2, BIAS_FAR,
                         jnp.where(j == i - 1, BIAS_NEAR, jnp.where(j == i, BIAS_DIAG, BIAS_MASKED)))

    def softmax(t, edge):
        for m in range(2):
            s = s_ref[m]
            if edge:
                s = jnp.concatenate(
                    [jnp.concatenate(
                        [s[g * tile:(g + 1) * tile, u * tile:(u + 1) * tile]
                         + bias_ref[0, m, tile_kind(t * group + g, i0 + u)] for u in range(ATTN_QTILES)], axis=1)
                     for g in range(group)], axis=0)
                m_cur = jnp.max(s, axis=0, keepdims=True)
            else:
                far_c = far_ref[2 * h + m]
                m_cur = jnp.max(s, axis=0, keepdims=True) + far_c
            m_prev = m_ref[m]
            m_new = jnp.maximum(m_prev, m_cur)
            alpha = jnp.exp2(m_prev - m_new)
            shift = m_new if edge else m_new - far_c
            p = jnp.exp2(s - shift)
            l_ref[m] = alpha * l_ref[m] + jnp.sum(p, axis=0, keepdims=True)
            p_ref[m] = p.astype(BF16)
            alpha_ref[m] = alpha
            m_ref[m] = m_new

    def values(t):
        vtblk = vt_ref[0, jnp.maximum(t, 0)]
        for m in range(2):
            acc_ref[m] = alpha_ref[m] * acc_ref[m] + jnp.dot(vtblk, p_ref[m], preferred_element_type=F32)

    @pl.when(qi == 0)
    def _():
        l_ref[...] = jnp.zeros_like(l_ref)
        acc_ref[...] = jnp.zeros_like(acc_ref)
        p_ref[...] = jnp.zeros_like(p_ref)
        logits(0, qi)

    m_ref[...] = jnp.full_like(m_ref, NEG_INF)
    alpha_ref[...] = jnp.ones_like(alpha_ref)

    def far_step(t):
        values(t - 1)
        softmax(t, edge=False)
        logits(t + 1, qi)

    def far_pair(u, carry):
        far_step(2 * u)
        far_step(2 * u + 1)
        return carry

    nfar = jnp.maximum(nsteps - 2, 0)
    lax.fori_loop(0, nfar // 2, far_pair, 0)

    @pl.when(nfar % 2 == 1)
    def _():
        far_step(nfar - 1)

    @pl.when(nsteps >= 2)
    def _():
        values(nsteps - 3)
        softmax(nsteps - 2, edge=True)
        logits(nsteps - 1, qi)

    values(nsteps - 2)
    softmax(nsteps - 1, edge=True)
    values(nsteps - 1)
    logits(0, jnp.minimum(qi + 1, pl.num_programs(2) - 1))

    o_t = acc_ref[0] / l_ref[0] - lam_ref[0] * (acc_ref[1] / l_ref[1])
    o_ref[...] = (_rms(o_t.T, sw_ref[...]) * out_scale).astype(BF16)


def _diff_attn(qa, qb, k, vt, bias_tiles, far, lam, subln_w, batch, out_scale):
    t, d = k.shape
    seq = t // batch
    nheads = d // HEAD
    tile = ATTN_TILE
    width = ATTN_GROUP * tile
    qrows = ATTN_QTILES * tile
    nq = seq // qrows
    smem = pl.BlockSpec(memory_space=pltpu.SMEM)
    head_spec = pl.BlockSpec((seq, HEAD), lambda b, h, i: (b, h))
    return pl.pallas_call(
        functools.partial(_attn_kernel, out_scale=out_scale),
        grid=(batch, nheads, nq),
        in_specs=[smem, smem, head_spec, head_spec, head_spec,
                  pl.BlockSpec((1, seq // width, HEAD, width), lambda b, h, i: (b * nheads + h, 0, 0, 0)),
                  pl.BlockSpec((1, 2, 4, tile, tile), lambda b, h, i: (h, 0, 0, 0, 0)),
                  pl.BlockSpec((1, HEAD), lambda b, h, i: (0, 0))],
        out_specs=pl.BlockSpec((qrows, HEAD), lambda b, h, i: (b * nq + i, h)),
        out_shape=jax.ShapeDtypeStruct((t, d), BF16),
        scratch_shapes=[pltpu.VMEM((2, 1, qrows), F32)] * 3
        + [pltpu.VMEM((2, HEAD, qrows), F32), pltpu.VMEM((2, width, qrows), F32),
           pltpu.VMEM((2, width, qrows), BF16)],
        compiler_params=_params("parallel", "parallel", "arbitrary"),
        name="diff_attn",
    )(lam.reshape(1).astype(F32), far, qa, qb, k, vt, bias_tiles, subln_w.reshape(1, HEAD).astype(F32))


def kernel(x, a_norm_pre, a_norm_post, a_w_in, a_lb, a_gate_norm, a_w_out, kv_norm, w_kv,
           b_norm_pre, b_norm_post, b_w_q, b_lambda, b_subln, b_w_out, rel_bias,
           mlp_norm_pre, mlp_norm_post, mlp_w_up, mlp_w_down):
    batch, seq, d = x.shape
    n_a = a_w_in.shape[0]
    n_b = b_w_q.shape[0]
    nheads = d // HEAD
    bf = lambda w: w.astype(BF16)

    lb_all = jnp.cumsum(jax.nn.softmax(a_lb.astype(F32), axis=0), axis=0)
    lb_all = lb_all - lb_all[0:1]

    h = x.reshape(batch * seq, d)
    for a in range(n_a):
        q, b, k, v, sg = _hgrn_in(h, a_norm_pre[a], bf(a_w_in[a]), lb_all[a])
        o = _hgrn_rec(q, b, k, v, sg, a_gate_norm[a], batch)
        h = _proj_mlp(o, bf(a_w_out[a]), h, a_norm_post[a],
                      mlp_norm_pre[a], bf(mlp_w_up[a]), bf(mlp_w_down[a]), mlp_norm_post[a])

    k_sh, v_sh = _norm_mm(h, kv_norm, bf(w_kv))
    width = ATTN_GROUP * ATTN_TILE
    vt = v_sh.reshape(batch, seq // width, width, nheads, HEAD).transpose(0, 3, 1, 4, 2)
    vt = vt.reshape(batch * nheads, seq // width, HEAD, width)
    bias_tiles, far = _bias_tables(rel_bias, nheads)
    q_scale = (HEAD // 2) ** -0.5 * LOG2E
    for bi in range(n_b):
        layer = n_a + bi
        lam_init = 0.8 - 0.6 * math.exp(-0.3 * layer)
        lp = b_lambda[bi].astype(F32)
        lam = jnp.exp(jnp.sum(lp[0] * lp[1])) - jnp.exp(jnp.sum(lp[2] * lp[3])) + lam_init
        qa, qb = _norm_mm(h, b_norm_pre[bi], bf(b_w_q[bi]), scale=q_scale, split_maps=True)
        o = _diff_attn(qa, qb, k_sh, vt, bias_tiles, far, lam, b_subln[bi], batch, 1.0 - lam_init)
        h = _proj_mlp(o, bf(b_w_out[bi]), h, b_norm_post[bi],
                      mlp_norm_pre[layer], bf(mlp_w_up[layer]), bf(mlp_w_down[layer]), mlp_norm_post[layer])
    return h.reshape(batch, seq, d)
```

```python
import functools
import math

import jax
import jax.numpy as jnp
from jax import lax
from jax.experimental import pallas as pl
from jax.experimental.pallas import tpu as pltpu

EPS = 1e-6
NEG_INF = -1e30
CHUNK = 64
HEAD = 128
REL_BUCKETS = 32
REL_MAX_DIST = 128
LOG2E = math.log2(math.e)

ROW_TILE = 512
REC_ROWS = 512
REC_SUB = 16
REC_UNROLL = 4
ATTN_TILE = 256
ATTN_ROWS = 64
ATTN_UNROLL = 2
ATTN_QTILES = 2
ATTN_GROUP = 2
BIAS_NEAR, BIAS_DIAG, BIAS_FAR, BIAS_MASKED = range(4)
VMEM_LIMIT = 56 * 1024 * 1024

F32 = jnp.float32
BF16 = jnp.bfloat16


def _rms(x, w):
    return x * lax.rsqrt(jnp.mean(x * x, axis=-1, keepdims=True) + EPS) * w


def _sigmoid(x):
    return 1.0 / (1.0 + jnp.exp(-x))


def _params(*sem):
    return pltpu.CompilerParams(dimension_semantics=sem, vmem_limit_bytes=VMEM_LIMIT)


def _resident(shape):
    return pl.BlockSpec(shape, lambda *_: (0,) * len(shape), pipeline_mode=pl.Buffered(1))


def _hgrn_in_kernel(h_ref, nw_ref, w_ref, loglb_ref, log1mlb_ref, onemlb_ref,
                    q_ref, b_ref, k_ref, v_ref, sg_ref):
    d = h_ref.shape[1]
    xn = _rms(h_ref[...], nw_ref[...]).astype(BF16)

    def proj(c):
        return jnp.dot(xn, w_ref[:, c * d:(c + 1) * d], preferred_element_type=F32)

    q = proj(0)
    q_ref[...] = (q * _sigmoid(q)).astype(BF16)
    f = proj(1)
    e = jnp.exp(-jnp.abs(f))
    inv = 1.0 / (1.0 + e)
    log_sig = jnp.minimum(f, 0.0) + jnp.log(inv)
    a = loglb_ref[...]
    c = log1mlb_ref[...] + log_sig
    log_f = jnp.maximum(a, c) + jnp.log(1.0 + jnp.exp(-jnp.abs(a - c)))
    b = log_f * LOG2E
    row_in_blk = lax.broadcasted_iota(jnp.int32, b.shape, 0) % REC_SUB
    shift = 1
    while shift < REC_SUB:
        b = b + jnp.where(row_in_blk >= shift, pltpu.roll(b, shift, axis=0), 0.0)
        shift *= 2
    b_ref[...] = b
    k_ref[...] = (onemlb_ref[...] * jnp.where(f > 0.0, e * inv, inv)).astype(BF16)
    v_ref[...] = proj(2).astype(BF16)
    g = proj(3)
    sg_ref[...] = (g * _sigmoid(g)).astype(BF16)


def _hgrn_in(h, nw, w, lb):
    t, d = h.shape
    row = lambda i: (i, 0)
    vec = lambda a: a.reshape(1, d).astype(F32)
    out_bf = jax.ShapeDtypeStruct((t, d), BF16)
    return pl.pallas_call(
        _hgrn_in_kernel,
        grid=(t // ROW_TILE,),
        in_specs=[pl.BlockSpec((ROW_TILE, d), row), _resident((1, d)), _resident(w.shape),
                  _resident((1, d)), _resident((1, d)), _resident((1, d))],
        out_specs=[pl.BlockSpec((ROW_TILE, d), row)] * 5,
        out_shape=[out_bf, jax.ShapeDtypeStruct((t, d), F32), out_bf, out_bf, out_bf],
        compiler_params=_params("parallel"),
        name="hgrn_in",
    )(h, vec(nw), w, vec(jnp.log(lb)), vec(jnp.log1p(-lb)), vec(1.0 - lb))


def _hgrn_rec_kernel(q_ref, b_ref, k_ref, v_ref, sg_ref, gnw_ref, o_ref, st_ref, *, nheads):
    sub = REC_SUB
    half = sub // 2
    nblk = REC_UNROLL
    chunk = nblk * sub

    @pl.when(pl.program_id(1) == 0)
    def _():
        st_ref[...] = jnp.zeros_like(st_ref)

    col = lax.broadcasted_iota(jnp.int32, (half, sub), 1)
    causal = (lax.broadcasted_iota(jnp.int32, (sub, sub), 1)
              <= lax.broadcasted_iota(jnp.int32, (sub, sub), 0))
    gnw = gnw_ref[...]
    heads = range(nheads)
    blocks = range(nblk)
    cs = [slice(h * HEAD, (h + 1) * HEAD) for h in heads]
    nt = (((1,), (1,)), ((), ()))
    tn = (((0,), (0,)), ((), ()))

    def body(trip, carry):
        base = pl.multiple_of(trip * chunk, chunk)
        rows_all = pl.ds(base, chunk)
        rows = [pl.ds(pl.multiple_of(base + j * sub, sub), sub) for j in blocks]
        b = [[b_ref[rows[j], cs[h]] for h in heads] for j in blocks]
        q = [[q_ref[rows[j], cs[h]].astype(F32) for h in heads] for j in blocks]

        inter = [None] * nheads
        cross = [[None] * nheads for _ in blocks]
        for h in heads:
            tot = [b[j][h][sub - 1:sub, :] for j in blocks]
            start = [None] * (nblk + 1)
            for j in range(1, nblk + 1):
                start[j] = tot[j - 1] if start[j - 1] is None else start[j - 1] + tot[j - 1]
            qe = [q[j][h] * jnp.exp2(b[j][h]) for j in blocks]
            kd = [k_ref[rows[j], cs[h]].astype(F32) * jnp.exp2(tot[j] - b[j][h]) for j in blocks]
            q_chunk = [qe[j] if start[j] is None else qe[j] * jnp.exp2(start[j]) for j in blocks]
            k_chunk = [kd[j] if j == nblk - 1 else kd[j] * jnp.exp2(start[nblk] - start[j + 1]) for j in blocks]
            inter[h] = lax.dot_general(jnp.concatenate(q_chunk, axis=0).astype(BF16), st_ref[h].astype(BF16),
                                       nt, preferred_element_type=F32)
            upd = lax.dot_general(v_ref[rows_all, cs[h]], jnp.concatenate(k_chunk, axis=0).astype(BF16),
                                  tn, preferred_element_type=F32)
            st_ref[h] = st_ref[h] * jnp.exp2(start[nblk]) + upd
            for i in range(1, nblk):
                keys = [kd[j] if j == i - 1 else kd[j] * jnp.exp2(start[i] - start[j + 1]) for j in range(i)]
                cross[i][h] = lax.dot_general(qe[i].astype(BF16), jnp.concatenate(keys, axis=0).astype(BF16),
                                              nt, preferred_element_type=F32)

        for j in blocks:
            raw = [None] * nheads
            for h in heads:
                lo = slice(0, half)
                hi = slice(half, sub)
                pieces = []
                for s in range(sub):
                    b_s = b[j][h][s:s + 1, :]
                    if s < half:
                        pieces.append(q[j][h][lo] * jnp.exp2(jnp.minimum(b[j][h][lo] - b_s, 0.0)))
                    pieces.append(q[j][h][hi] * jnp.exp2(jnp.minimum(b[j][h][hi] - b_s, 0.0)))
                stacked = jnp.concatenate(pieces, axis=0).astype(BF16)
                raw[h] = lax.dot_general(stacked, k_ref[rows[j], cs[h]], nt, preferred_element_type=F32)

            for h in heads:
                s_lo = jnp.zeros((half, sub), F32)
                s_hi = jnp.zeros((half, sub), F32)
                off = 0
                for s in range(sub):
                    if s < half:
                        s_lo = jnp.where(col == s, raw[h][off:off + half], s_lo)
                        off += half
                    s_hi = jnp.where(col == s, raw[h][off:off + half], s_hi)
                    off += half
                scores = jnp.where(causal, jnp.concatenate([s_lo, s_hi], axis=0), 0.0).astype(BF16)
                o = inter[h][j * sub:(j + 1) * sub] + jnp.dot(scores, v_ref[rows[j], cs[h]],
                                                              preferred_element_type=F32)
                if j > 0:
                    o = o + jnp.dot(cross[j][h].astype(BF16), v_ref[pl.ds(base, j * sub), cs[h]],
                                    preferred_element_type=F32)
                og = _rms(o, gnw) * sg_ref[rows[j], cs[h]].astype(F32)
                o_ref[rows[j], cs[h]] = og.astype(BF16)
        return carry

    lax.fori_loop(0, q_ref.shape[0] // chunk, body, 0)


def _hgrn_rec(q, b, k, v, sg, gnw, batch):
    t, d = q.shape
    nheads = d // HEAD
    nt = t // batch // REC_ROWS
    row = lambda b, i: (b * nt + i, 0)
    blk = pl.BlockSpec((REC_ROWS, d), row)
    return pl.pallas_call(
        functools.partial(_hgrn_rec_kernel, nheads=nheads),
        grid=(batch, nt),
        in_specs=[blk, blk, blk, blk, blk, _resident((1, HEAD))],
        out_specs=blk,
        out_shape=jax.ShapeDtypeStruct((t, d), BF16),
        scratch_shapes=[pltpu.VMEM((nheads, HEAD, HEAD), F32)],
        compiler_params=_params("parallel", "arbitrary"),
        name="hgrn_rec",
    )(q, b, k, v, sg, gnw.reshape(1, HEAD).astype(F32))


def _proj_mlp_kernel(x_ref, wo_ref, h_ref, npost_ref, n1_ref, wu_ref, wd_ref, n2_ref, o_ref, u_ref):
    mix = jnp.dot(x_ref[...], wo_ref[...], preferred_element_type=F32)
    h = h_ref[...] + _rms(mix, npost_ref[...])
    d = h.shape[1]
    xn = _rms(h, n1_ref[...]).astype(BF16)
    for c in range(wu_ref.shape[1] // d):
        u = jnp.maximum(jnp.dot(xn, wu_ref[:, c * d:(c + 1) * d], preferred_element_type=F32), 0.0)
        u_ref[:, c * d:(c + 1) * d] = (u * u).astype(BF16)
    y = jnp.dot(u_ref[...], wd_ref[...], preferred_element_type=F32)
    o_ref[...] = h + _rms(y, n2_ref[...])


def _proj_mlp(x, wo, h, npost, n1, wu, wd, n2):
    t, d = h.shape
    row = lambda i: (i, 0)
    vec = lambda a: a.reshape(1, d).astype(F32)
    blk = pl.BlockSpec((ROW_TILE, d), row)
    return pl.pallas_call(
        _proj_mlp_kernel,
        grid=(t // ROW_TILE,),
        in_specs=[blk, _resident(wo.shape), blk, _resident((1, d)), _resident((1, d)), _resident(wu.shape),
                  _resident(wd.shape), _resident((1, d))],
        out_specs=blk,
        out_shape=jax.ShapeDtypeStruct((t, d), F32),
        scratch_shapes=[pltpu.VMEM((ROW_TILE, wu.shape[1]), BF16)],
        compiler_params=_params("parallel"),
        name="proj_mlp",
    )(x, wo, h, vec(npost), vec(n1), wu, wd, vec(n2))


def _norm_mm_kernel(h_ref, nw_ref, w_ref, *o_refs, scale, split_maps):
    d = h_ref.shape[1]
    xn = _rms(h_ref[...], nw_ref[...]).astype(BF16)
    if split_maps:
        y = (jnp.dot(xn, w_ref[...], preferred_element_type=F32) * scale).astype(BF16)
        first = lax.broadcasted_iota(jnp.int32, y.shape, 1) % HEAD < HEAD // 2
        zero = jnp.zeros_like(y)
        o_refs[0][...] = jnp.where(first, y, zero)
        o_refs[1][...] = jnp.where(first, zero, y)
        return
    for c, o_ref in enumerate(o_refs):
        y = jnp.dot(xn, w_ref[:, c * d:(c + 1) * d], preferred_element_type=F32)
        o_ref[...] = (y * scale).astype(BF16)


def _norm_mm(h, nw, w, scale=1.0, split_maps=False):
    t, d = h.shape
    n_out = 2 if split_maps else w.shape[1] // d
    row = lambda i: (i, 0)
    return pl.pallas_call(
        functools.partial(_norm_mm_kernel, scale=scale, split_maps=split_maps),
        grid=(t // ROW_TILE,),
        in_specs=[pl.BlockSpec((ROW_TILE, d), row), _resident((1, d)), _resident(w.shape)],
        out_specs=[pl.BlockSpec((ROW_TILE, d), row)] * n_out,
        out_shape=[jax.ShapeDtypeStruct((t, d), BF16)] * n_out,
        compiler_params=_params("parallel"),
        name="norm_mm",
    )(h, nw.reshape(1, d).astype(F32), w)


def _rel_bucket(rel):
    half = REL_BUCKETS // 2
    max_exact = half // 2
    ret = jnp.where(rel > 0, half, 0)
    n = jnp.abs(rel)
    nf = jnp.maximum(n, 1).astype(jnp.float32)
    large = max_exact + (jnp.log(nf / max_exact) / math.log(REL_MAX_DIST / max_exact)
                         * (half - max_exact)).astype(jnp.int32)
    large = jnp.minimum(large, half - 1)
    return ret + jnp.where(n < max_exact, n, large)


def _bias_tables(rel_bias, nheads):
    tile = ATTN_TILE
    kp = jnp.arange(tile)[:, None]
    qp = jnp.arange(tile)[None, :]
    table = rel_bias.astype(F32) * LOG2E

    def lookup(rel):
        onehot = (_rel_bucket(rel)[..., None] == jnp.arange(REL_BUCKETS)).astype(F32)
        b = jnp.dot(onehot, table, precision=lax.Precision.HIGHEST)
        return b.reshape(tile, tile, nheads, 2).transpose(2, 3, 0, 1)

    far = table[_rel_bucket(jnp.int32(-(tile + 1)))]
    kinds = [None] * 4
    kinds[BIAS_NEAR] = lookup(kp - tile - qp)
    kinds[BIAS_DIAG] = jnp.where(kp // CHUNK <= qp // CHUNK, lookup(kp - qp), NEG_INF)
    kinds[BIAS_FAR] = jnp.broadcast_to(far.reshape(nheads, 2, 1, 1), (nheads, 2, tile, tile))
    kinds[BIAS_MASKED] = jnp.full((nheads, 2, tile, tile), NEG_INF, F32)
    return jnp.stack(kinds, axis=2), far


def _attn_kernel(lam_ref, far_ref, qa_ref, qb_ref, k_ref, vt_ref, bias_ref, sw_ref, o_ref,
                 m_ref, l_ref, alpha_ref, acc_ref, s_ref, p_ref, *, out_scale):
    h = pl.program_id(1)
    qi = pl.program_id(2)
    i0 = qi * ATTN_QTILES
    group = ATTN_GROUP
    tile = ATTN_TILE
    width = group * tile
    qrows = ATTN_QTILES * tile
    nsteps = (i0 + ATTN_QTILES - 1) // group + 1

    def logits(t, q_index):
        start = pl.multiple_of(t * width, width)
        kblk = k_ref[pl.ds(start, width), :]
        q_start = pl.multiple_of(q_index * qrows, qrows)
        for m, q_ref in enumerate((qa_ref, qb_ref)):
            s_ref[m] = lax.dot_general(kblk, q_ref[pl.ds(q_start, qrows), :], (((1,), (1,)), ((), ())),
                                       preferred_element_type=F32)

    def tile_kind(j, i):
        return jnp.where(j <= i - 2, BIAS_FAR,
                         jnp.where(j == i - 1, BIAS_NEAR, jnp.where(j == i, BIAS_DIAG, BIAS_MASKED)))

    fold = lambda x, op: op(x.reshape(x.shape[0] // 8, 8, x.shape[1]), axis=0)
    pieces = range(width // ATTN_ROWS)

    def softmax(t, edge):
        for m in range(2):
            if edge:
                s = s_ref[m]
                s = jnp.concatenate(
                    [jnp.concatenate(
                        [s[g * tile:(g + 1) * tile, u * tile:(u + 1) * tile]
                         + bias_ref[0, m, tile_kind(t * group + g, i0 + u)] for u in range(ATTN_QTILES)], axis=1)
                     for g in range(group)], axis=0)
                m_cur = jnp.max(s, axis=0, keepdims=True)
            else:
                far_c = far_ref[2 * h + m]
                col_max = None
                for c in pieces:
                    part = fold(s_ref[m, c * ATTN_ROWS:(c + 1) * ATTN_ROWS, :], jnp.max)
                    col_max = part if col_max is None else jnp.maximum(col_max, part)
                m_cur = jnp.max(col_max, axis=0, keepdims=True) + far_c
            m_prev = m_ref[m]
            m_new = jnp.maximum(m_prev, m_cur)
            alpha = jnp.exp2(m_prev - m_new)
            if edge:
                p = jnp.exp2(s - m_new)
                p_ref[m] = p.astype(BF16)
                p_sum = jnp.sum(p, axis=0, keepdims=True)
            else:
                col_sum = None
                for c in pieces:
                    rows = slice(c * ATTN_ROWS, (c + 1) * ATTN_ROWS)
                    p = jnp.exp2(s_ref[m, rows, :] - (m_new - far_c))
                    p_ref[m, rows, :] = p.astype(BF16)
                    part = fold(p, jnp.sum)
                    col_sum = part if col_sum is None else col_sum + part
                p_sum = jnp.sum(col_sum, axis=0, keepdims=True)
            l_ref[m] = alpha * l_ref[m] + p_sum
            alpha_ref[m] = alpha
            m_ref[m] = m_new

    def values(t):
        vtblk = vt_ref[0, jnp.maximum(t, 0)]
        for m in range(2):
            acc_ref[m] = alpha_ref[m] * acc_ref[m] + jnp.dot(vtblk, p_ref[m], preferred_element_type=F32)

    @pl.when(qi == 0)
    def _():
        l_ref[...] = jnp.zeros_like(l_ref)
        acc_ref[...] = jnp.zeros_like(acc_ref)
        p_ref[...] = jnp.zeros_like(p_ref)
        logits(0, qi)

    m_ref[...] = jnp.full_like(m_ref, NEG_INF)
    alpha_ref[...] = jnp.ones_like(alpha_ref)

    def far_step(t):
        values(t - 1)
        softmax(t, edge=False)
        logits(t + 1, qi)

    def far_group(u, carry):
        for j in range(ATTN_UNROLL):
            far_step(ATTN_UNROLL * u + j)
        return carry

    def far_single(t, carry):
        far_step(t)
        return carry

    nfar = jnp.maximum(nsteps - 2, 0)
    ngroups = nfar // ATTN_UNROLL
    lax.fori_loop(0, ngroups, far_group, 0)
    lax.fori_loop(ngroups * ATTN_UNROLL, nfar, far_single, 0)

    @pl.when(nsteps >= 2)
    def _():
        values(nsteps - 3)
        softmax(nsteps - 2, edge=True)
        logits(nsteps - 1, qi)

    values(nsteps - 2)
    softmax(nsteps - 1, edge=True)
    values(nsteps - 1)
    logits(0, jnp.minimum(qi + 1, pl.num_programs(2) - 1))

    o_t = acc_ref[0] / l_ref[0] - lam_ref[0] * (acc_ref[1] / l_ref[1])
    o_ref[...] = (_rms(o_t.T, sw_ref[...]) * out_scale).astype(BF16)


def _diff_attn(qa, qb, k, vt, bias_tiles, far, lam, subln_w, batch, out_scale):
    t, d = k.shape
    seq = t // batch
    nheads = d // HEAD
    tile = ATTN_TILE
    width = ATTN_GROUP * tile
    qrows = ATTN_QTILES * tile
    nq = seq // qrows
    smem = pl.BlockSpec(memory_space=pltpu.SMEM)
    head_spec = pl.BlockSpec((seq, HEAD), lambda b, h, i: (b, h))
    return pl.pallas_call(
        functools.partial(_attn_kernel, out_scale=out_scale),
        grid=(batch, nheads, nq),
        in_specs=[smem, smem, head_spec, head_spec, head_spec,
                  pl.BlockSpec((1, seq // width, HEAD, width), lambda b, h, i: (b * nheads + h, 0, 0, 0)),
                  pl.BlockSpec((1, 2, 4, tile, tile), lambda b, h, i: (h, 0, 0, 0, 0)),
                  pl.BlockSpec((1, HEAD), lambda b, h, i: (0, 0))],
        out_specs=pl.BlockSpec((qrows, HEAD), lambda b, h, i: (b * nq + i, h)),
        out_shape=jax.ShapeDtypeStruct((t, d), BF16),
        scratch_shapes=[pltpu.VMEM((2, 1, qrows), F32)] * 3
        + [pltpu.VMEM((2, HEAD, qrows), F32), pltpu.VMEM((2, width, qrows), F32),
           pltpu.VMEM((2, width, qrows), BF16)],
        compiler_params=_params("parallel", "parallel", "arbitrary"),
        name="diff_attn",
    )(lam.reshape(1).astype(F32), far, qa, qb, k, vt, bias_tiles, subln_w.reshape(1, HEAD).astype(F32))


def kernel(x, a_norm_pre, a_norm_post, a_w_in, a_lb, a_gate_norm, a_w_out, kv_norm, w_kv,
           b_norm_pre, b_norm_post, b_w_q, b_lambda, b_subln, b_w_out, rel_bias,
           mlp_norm_pre, mlp_norm_post, mlp_w_up, mlp_w_down):
    batch, seq, d = x.shape
    n_a = a_w_in.shape[0]
    n_b = b_w_q.shape[0]
    nheads = d // HEAD
    bf = lambda w: w.astype(BF16)

    lb_all = jnp.cumsum(jax.nn.softmax(a_lb.astype(F32), axis=0), axis=0)
    lb_all = lb_all - lb_all[0:1]

    h = x.reshape(batch * seq, d)
    for a in range(n_a):
        q, b, k, v, sg = _hgrn_in(h, a_norm_pre[a], bf(a_w_in[a]), lb_all[a])
        o = _hgrn_rec(q, b, k, v, sg, a_gate_norm[a], batch)
        h = _proj_mlp(o, bf(a_w_out[a]), h, a_norm_post[a],
                      mlp_norm_pre[a], bf(mlp_w_up[a]), bf(mlp_w_down[a]), mlp_norm_post[a])

    k_sh, v_sh = _norm_mm(h, kv_norm, bf(w_kv))
    width = ATTN_GROUP * ATTN_TILE
    vt = v_sh.reshape(batch, seq // width, width, nheads, HEAD).transpose(0, 3, 1, 4, 2)
    vt = vt.reshape(batch * nheads, seq // width, HEAD, width)
    bias_tiles, far = _bias_tables(rel_bias, nheads)
    q_scale = (HEAD // 2) ** -0.5 * LOG2E
    for bi in range(n_b):
        layer = n_a + bi
        lam_init = 0.8 - 0.6 * math.exp(-0.3 * layer)
        lp = b_lambda[bi].astype(F32)
        lam = jnp.exp(jnp.sum(lp[0] * lp[1])) - jnp.exp(jnp.sum(lp[2] * lp[3])) + lam_init
        qa, qb = _norm_mm(h, b_norm_pre[bi], bf(b_w_q[bi]), scale=q_scale, split_maps=True)
        o = _diff_attn(qa, qb, k_sh, vt, bias_tiles, far, lam, b_subln[bi], batch, 1.0 - lam_init)
        h = _proj_mlp(o, bf(b_w_out[bi]), h, b_norm_post[bi],
                      mlp_norm_pre[layer], bf(mlp_w_up[layer]), bf(mlp_w_down[layer]), mlp_norm_post[layer])
    return h.reshape(batch, seq, d)
```

```python
import functools
import math

import jax
import jax.numpy as jnp
from jax import lax
from jax.experimental import pallas as pl
from jax.experimental.pallas import tpu as pltpu

EPS = 1e-6
NEG_INF = -1e30
CHUNK = 64
HEAD = 128
REL_BUCKETS = 32
REL_MAX_DIST = 128
LOG2E = math.log2(math.e)

ROW_TILE = 1024
REC_ROWS = 512
REC_SUB = 16
REC_UNROLL = 4
ATTN_TILE = 256
ATTN_ROWS = 64
ATTN_UNROLL = 2
ATTN_PAD = 128
ATTN_QTILES = 2
ATTN_GROUP = 2
BIAS_NEAR, BIAS_DIAG, BIAS_FAR, BIAS_MASKED = range(4)
VMEM_LIMIT = 56 * 1024 * 1024

F32 = jnp.float32
BF16 = jnp.bfloat16


def _rms(x, w):
    return x * lax.rsqrt(jnp.mean(x * x, axis=-1, keepdims=True) + EPS) * w


def _sigmoid(x):
    return 1.0 / (1.0 + jnp.exp(-x))


def _params(*sem):
    return pltpu.CompilerParams(dimension_semantics=sem, vmem_limit_bytes=VMEM_LIMIT)


def _resident(shape):
    return pl.BlockSpec(shape, lambda *_: (0,) * len(shape), pipeline_mode=pl.Buffered(1))


def _hgrn_in_kernel(h_ref, nw_ref, w_ref, loglb_ref, log1mlb_ref, onemlb_ref,
                    q_ref, b_ref, k_ref, v_ref, sg_ref):
    d = h_ref.shape[1]
    xn = _rms(h_ref[...], nw_ref[...]).astype(BF16)

    def proj(c):
        return jnp.dot(xn, w_ref[:, c * d:(c + 1) * d], preferred_element_type=F32)

    q = proj(0)
    q_ref[...] = (q * _sigmoid(q)).astype(BF16)
    f = proj(1)
    e = jnp.exp(-jnp.abs(f))
    inv = 1.0 / (1.0 + e)
    log_sig = jnp.minimum(f, 0.0) + jnp.log(inv)
    a = loglb_ref[...]
    c = log1mlb_ref[...] + log_sig
    log_f = jnp.maximum(a, c) + jnp.log(1.0 + jnp.exp(-jnp.abs(a - c)))
    b = log_f * LOG2E
    row_in_blk = lax.broadcasted_iota(jnp.int32, b.shape, 0) % REC_SUB
    shift = 1
    while shift < REC_SUB:
        b = b + jnp.where(row_in_blk >= shift, pltpu.roll(b, shift, axis=0), 0.0)
        shift *= 2
    b_ref[...] = b
    k_ref[...] = (onemlb_ref[...] * jnp.where(f > 0.0, e * inv, inv)).astype(BF16)
    v_ref[...] = proj(2).astype(BF16)
    g = proj(3)
    sg_ref[...] = (g * _sigmoid(g)).astype(BF16)


def _hgrn_in(h, nw, w, lb):
    t, d = h.shape
    row = lambda i: (i, 0)
    vec = lambda a: a.reshape(1, d).astype(F32)
    out_bf = jax.ShapeDtypeStruct((t, d), BF16)
    return pl.pallas_call(
        _hgrn_in_kernel,
        grid=(t // ROW_TILE,),
        in_specs=[pl.BlockSpec((ROW_TILE, d), row), _resident((1, d)), _resident(w.shape),
                  _resident((1, d)), _resident((1, d)), _resident((1, d))],
        out_specs=[pl.BlockSpec((ROW_TILE, d), row)] * 5,
        out_shape=[out_bf, jax.ShapeDtypeStruct((t, d), F32), out_bf, out_bf, out_bf],
        compiler_params=_params("parallel"),
        name="hgrn_in",
    )(h, vec(nw), w, vec(jnp.log(lb)), vec(jnp.log1p(-lb)), vec(1.0 - lb))


def _hgrn_rec_kernel(q_ref, b_ref, k_ref, v_ref, sg_ref, gnw_ref, o_ref, st_ref, *, nheads):
    sub = REC_SUB
    half = sub // 2
    nblk = REC_UNROLL
    chunk = nblk * sub

    @pl.when(pl.program_id(1) == 0)
    def _():
        st_ref[...] = jnp.zeros_like(st_ref)

    col = lax.broadcasted_iota(jnp.int32, (half, sub), 1)
    causal = (lax.broadcasted_iota(jnp.int32, (sub, sub), 1)
              <= lax.broadcasted_iota(jnp.int32, (sub, sub), 0))
    gnw = gnw_ref[...]
    heads = range(nheads)
    blocks = range(nblk)
    cs = [slice(h * HEAD, (h + 1) * HEAD) for h in heads]
    nt = (((1,), (1,)), ((), ()))
    tn = (((0,), (0,)), ((), ()))

    def body(trip, carry):
        base = pl.multiple_of(trip * chunk, chunk)
        rows_all = pl.ds(base, chunk)
        rows = [pl.ds(pl.multiple_of(base + j * sub, sub), sub) for j in blocks]
        b = [[b_ref[rows[j], cs[h]] for h in heads] for j in blocks]
        q = [[q_ref[rows[j], cs[h]].astype(F32) for h in heads] for j in blocks]

        inter = [None] * nheads
        cross = [[None] * nheads for _ in blocks]
        for h in heads:
            tot = [b[j][h][sub - 1:sub, :] for j in blocks]
            start = [None] * (nblk + 1)
            for j in range(1, nblk + 1):
                start[j] = tot[j - 1] if start[j - 1] is None else start[j - 1] + tot[j - 1]
            qe = [q[j][h] * jnp.exp2(b[j][h]) for j in blocks]
            kd = [k_ref[rows[j], cs[h]].astype(F32) * jnp.exp2(tot[j] - b[j][h]) for j in blocks]
            q_chunk = [qe[j] if start[j] is None else qe[j] * jnp.exp2(start[j]) for j in blocks]
            k_chunk = [kd[j] if j == nblk - 1 else kd[j] * jnp.exp2(start[nblk] - start[j + 1]) for j in blocks]
            inter[h] = lax.dot_general(jnp.concatenate(q_chunk, axis=0).astype(BF16), st_ref[h].astype(BF16),
                                       nt, preferred_element_type=F32)
            upd = lax.dot_general(v_ref[rows_all, cs[h]], jnp.concatenate(k_chunk, axis=0).astype(BF16),
                                  tn, preferred_element_type=F32)
            st_ref[h] = st_ref[h] * jnp.exp2(start[nblk]) + upd
            for i in range(1, nblk):
                keys = [kd[j] if j == i - 1 else kd[j] * jnp.exp2(start[i] - start[j + 1]) for j in range(i)]
                cross[i][h] = lax.dot_general(qe[i].astype(BF16), jnp.concatenate(keys, axis=0).astype(BF16),
                                              nt, preferred_element_type=F32)

        for j in blocks:
            raw = [None] * nheads
            for h in heads:
                lo = slice(0, half)
                hi = slice(half, sub)
                pieces = []
                for s in range(sub):
                    b_s = b[j][h][s:s + 1, :]
                    if s < half:
                        pieces.append(q[j][h][lo] * jnp.exp2(jnp.minimum(b[j][h][lo] - b_s, 0.0)))
                    pieces.append(q[j][h][hi] * jnp.exp2(jnp.minimum(b[j][h][hi] - b_s, 0.0)))
                stacked = jnp.concatenate(pieces, axis=0).astype(BF16)
                raw[h] = lax.dot_general(stacked, k_ref[rows[j], cs[h]], nt, preferred_element_type=F32)

            for h in heads:
                s_lo = jnp.zeros((half, sub), F32)
                s_hi = jnp.zeros((half, sub), F32)
                off = 0
                for s in range(sub):
                    if s < half:
                        s_lo = jnp.where(col == s, raw[h][off:off + half], s_lo)
                        off += half
                    s_hi = jnp.where(col == s, raw[h][off:off + half], s_hi)
                    off += half
                scores = jnp.where(causal, jnp.concatenate([s_lo, s_hi], axis=0), 0.0).astype(BF16)
                o = inter[h][j * sub:(j + 1) * sub] + jnp.dot(scores, v_ref[rows[j], cs[h]],
                                                              preferred_element_type=F32)
                if j > 0:
                    o = o + jnp.dot(cross[j][h].astype(BF16), v_ref[pl.ds(base, j * sub), cs[h]],
                                    preferred_element_type=F32)
                og = _rms(o, gnw) * sg_ref[rows[j], cs[h]].astype(F32)
                o_ref[rows[j], cs[h]] = og.astype(BF16)
        return carry

    lax.fori_loop(0, q_ref.shape[0] // chunk, body, 0)


def _hgrn_rec(q, b, k, v, sg, gnw, batch):
    t, d = q.shape
    nheads = d // HEAD
    nt = t // batch // REC_ROWS
    row = lambda b, i: (b * nt + i, 0)
    blk = pl.BlockSpec((REC_ROWS, d), row)
    return pl.pallas_call(
        functools.partial(_hgrn_rec_kernel, nheads=nheads),
        grid=(batch, nt),
        in_specs=[blk, blk, blk, blk, blk, _resident((1, HEAD))],
        out_specs=blk,
        out_shape=jax.ShapeDtypeStruct((t, d), BF16),
        scratch_shapes=[pltpu.VMEM((nheads, HEAD, HEAD), F32)],
        compiler_params=_params("parallel", "arbitrary"),
        name="hgrn_rec",
    )(q, b, k, v, sg, gnw.reshape(1, HEAD).astype(F32))


def _proj_mlp_kernel(x_ref, wo_ref, h_ref, npost_ref, n1_ref, wu_ref, wd_ref, n2_ref, o_ref, u_ref):
    mix = jnp.dot(x_ref[...], wo_ref[...], preferred_element_type=F32)
    h = h_ref[...] + _rms(mix, npost_ref[...])
    d = h.shape[1]
    xn = _rms(h, n1_ref[...]).astype(BF16)
    for c in range(wu_ref.shape[1] // d):
        u = jnp.maximum(jnp.dot(xn, wu_ref[:, c * d:(c + 1) * d], preferred_element_type=F32), 0.0)
        u_ref[:, c * d:(c + 1) * d] = (u * u).astype(BF16)
    y = jnp.dot(u_ref[...], wd_ref[...], preferred_element_type=F32)
    o_ref[...] = h + _rms(y, n2_ref[...])


def _proj_mlp(x, wo, h, npost, n1, wu, wd, n2):
    t, d = h.shape
    row = lambda i: (i, 0)
    vec = lambda a: a.reshape(1, d).astype(F32)
    blk = pl.BlockSpec((ROW_TILE, d), row)
    return pl.pallas_call(
        _proj_mlp_kernel,
        grid=(t // ROW_TILE,),
        in_specs=[blk, _resident(wo.shape), blk, _resident((1, d)), _resident((1, d)), _resident(wu.shape),
                  _resident(wd.shape), _resident((1, d))],
        out_specs=blk,
        out_shape=jax.ShapeDtypeStruct((t, d), F32),
        scratch_shapes=[pltpu.VMEM((ROW_TILE, wu.shape[1]), BF16)],
        compiler_params=_params("parallel"),
        name="proj_mlp",
    )(x, wo, h, vec(npost), vec(n1), wu, wd, vec(n2))


def _norm_mm_kernel(h_ref, nw_ref, w_ref, *o_refs, scale, split_maps):
    d = h_ref.shape[1]
    xn = _rms(h_ref[...], nw_ref[...]).astype(BF16)
    if split_maps:
        y = (jnp.dot(xn, w_ref[...], preferred_element_type=F32) * scale).astype(BF16)
        first = lax.broadcasted_iota(jnp.int32, y.shape, 1) % HEAD < HEAD // 2
        zero = jnp.zeros_like(y)
        o_refs[0][...] = jnp.where(first, y, zero)
        o_refs[1][...] = jnp.where(first, zero, y)
        return
    for c, o_ref in enumerate(o_refs):
        y = jnp.dot(xn, w_ref[:, c * d:(c + 1) * d], preferred_element_type=F32)
        o_ref[...] = (y * scale).astype(BF16)


def _norm_mm(h, nw, w, scale=1.0, split_maps=False):
    t, d = h.shape
    n_out = 2 if split_maps else w.shape[1] // d
    row = lambda i: (i, 0)
    return pl.pallas_call(
        functools.partial(_norm_mm_kernel, scale=scale, split_maps=split_maps),
        grid=(t // ROW_TILE,),
        in_specs=[pl.BlockSpec((ROW_TILE, d), row), _resident((1, d)), _resident(w.shape)],
        out_specs=[pl.BlockSpec((ROW_TILE, d), row)] * n_out,
        out_shape=[jax.ShapeDtypeStruct((t, d), BF16)] * n_out,
        compiler_params=_params("parallel"),
        name="norm_mm",
    )(h, nw.reshape(1, d).astype(F32), w)


def _rel_bucket(rel):
    half = REL_BUCKETS // 2
    max_exact = half // 2
    ret = jnp.where(rel > 0, half, 0)
    n = jnp.abs(rel)
    nf = jnp.maximum(n, 1).astype(jnp.float32)
    large = max_exact + (jnp.log(nf / max_exact) / math.log(REL_MAX_DIST / max_exact)
                         * (half - max_exact)).astype(jnp.int32)
    large = jnp.minimum(large, half - 1)
    return ret + jnp.where(n < max_exact, n, large)


def _bias_tables(rel_bias, nheads):
    tile = ATTN_TILE
    kp = jnp.arange(tile)[:, None]
    qp = jnp.arange(tile)[None, :]
    table = rel_bias.astype(F32) * LOG2E

    def lookup(rel):
        onehot = (_rel_bucket(rel)[..., None] == jnp.arange(REL_BUCKETS)).astype(F32)
        b = jnp.dot(onehot, table, precision=lax.Precision.HIGHEST)
        return b.reshape(tile, tile, nheads, 2).transpose(2, 3, 0, 1)

    far = table[_rel_bucket(jnp.int32(-(tile + 1)))]
    kinds = [None] * 4
    kinds[BIAS_NEAR] = lookup(kp - tile - qp)
    kinds[BIAS_DIAG] = jnp.where(kp // CHUNK <= qp // CHUNK, lookup(kp - qp), NEG_INF)
    kinds[BIAS_FAR] = jnp.broadcast_to(far.reshape(nheads, 2, 1, 1), (nheads, 2, tile, tile))
    kinds[BIAS_MASKED] = jnp.full((nheads, 2, tile, tile), NEG_INF, F32)
    return jnp.stack(kinds, axis=2), far


def _attn_kernel(lam_ref, far_ref, qa_ref, qb_ref, k_ref, vt_ref, bias_ref, sw_ref, o_ref,
                 m_ref, l_ref, alpha_ref, acc_ref, s_ref, p_ref, *, out_scale):
    h = pl.program_id(1)
    qi = pl.program_id(2)
    i0 = qi * ATTN_QTILES
    group = ATTN_GROUP
    tile = ATTN_TILE
    width = group * tile
    qrows = ATTN_QTILES * tile
    nsteps = (i0 + ATTN_QTILES - 1) // group + 1

    def logits(t, q_index):
        start = pl.multiple_of(t * width, width)
        kblk = k_ref[pl.ds(start, width), :]
        q_start = pl.multiple_of(q_index * qrows, qrows)
        for m, q_ref in enumerate((qa_ref, qb_ref)):
            s_ref[m, :, 0:qrows] = lax.dot_general(kblk, q_ref[pl.ds(q_start, qrows), :], (((1,), (1,)), ((), ())),
                                                   preferred_element_type=F32)

    def tile_kind(j, i):
        return jnp.where(j <= i - 2, BIAS_FAR,
                         jnp.where(j == i - 1, BIAS_NEAR, jnp.where(j == i, BIAS_DIAG, BIAS_MASKED)))

    fold = lambda x, op: op(x.reshape(x.shape[0] // 8, 8, x.shape[1]), axis=0)
    pieces = range(width // ATTN_ROWS)

    def softmax(t, edge):
        for m in range(2):
            if edge:
                s = s_ref[m, :, 0:qrows]
                s = jnp.concatenate(
                    [jnp.concatenate(
                        [s[g * tile:(g + 1) * tile, u * tile:(u + 1) * tile]
                         + bias_ref[0, m, tile_kind(t * group + g, i0 + u)] for u in range(ATTN_QTILES)], axis=1)
                     for g in range(group)], axis=0)
                m_cur = jnp.max(s, axis=0, keepdims=True)
            else:
                far_c = far_ref[2 * h + m]
                col_max = None
                for c in pieces:
                    part = fold(s_ref[m, c * ATTN_ROWS:(c + 1) * ATTN_ROWS, 0:qrows], jnp.max)
                    col_max = part if col_max is None else jnp.maximum(col_max, part)
                m_cur = jnp.max(col_max, axis=0, keepdims=True) + far_c
            m_prev = m_ref[m]
            m_new = jnp.maximum(m_prev, m_cur)
            alpha = jnp.exp2(m_prev - m_new)
            if edge:
                p = jnp.exp2(s - m_new)
                p_ref[m, :, 0:qrows] = p.astype(BF16)
                p_sum = jnp.sum(p, axis=0, keepdims=True)
            else:
                col_sum = None
                for c in pieces:
                    rows = slice(c * ATTN_ROWS, (c + 1) * ATTN_ROWS)
                    p = jnp.exp2(s_ref[m, rows, 0:qrows] - (m_new - far_c))
                    p_ref[m, rows, 0:qrows] = p.astype(BF16)
                    part = fold(p, jnp.sum)
                    col_sum = part if col_sum is None else col_sum + part
                p_sum = jnp.sum(col_sum, axis=0, keepdims=True)
            l_ref[m] = alpha * l_ref[m] + p_sum
            alpha_ref[m] = alpha
            m_ref[m] = m_new

    def values(t):
        vtblk = vt_ref[0, jnp.maximum(t, 0)]
        for m in range(2):
            acc_ref[m, :, 0:qrows] = (alpha_ref[m] * acc_ref[m, :, 0:qrows]
                                      + jnp.dot(vtblk, p_ref[m, :, 0:qrows], preferred_element_type=F32))

    @pl.when(qi == 0)
    def _():
        l_ref[...] = jnp.zeros_like(l_ref)
        acc_ref[...] = jnp.zeros_like(acc_ref)
        p_ref[...] = jnp.zeros_like(p_ref)
        logits(0, qi)

    m_ref[...] = jnp.full_like(m_ref, NEG_INF)
    alpha_ref[...] = jnp.ones_like(alpha_ref)

    def far_step(t):
        values(t - 1)
        softmax(t, edge=False)
        logits(t + 1, qi)

    def far_group(u, carry):
        for j in range(ATTN_UNROLL):
            far_step(ATTN_UNROLL * u + j)
        return carry

    def far_single(t, carry):
        far_step(t)
        return carry

    nfar = jnp.maximum(nsteps - 2, 0)
    ngroups = nfar // ATTN_UNROLL
    lax.fori_loop(0, ngroups, far_group, 0)
    lax.fori_loop(ngroups * ATTN_UNROLL, nfar, far_single, 0)

    @pl.when(nsteps >= 2)
    def _():
        values(nsteps - 3)
        softmax(nsteps - 2, edge=True)
        logits(nsteps - 1, qi)

    values(nsteps - 2)
    softmax(nsteps - 1, edge=True)
    values(nsteps - 1)
    logits(0, jnp.minimum(qi + 1, pl.num_programs(2) - 1))

    o_t = acc_ref[0, :, 0:qrows] / l_ref[0] - lam_ref[0] * (acc_ref[1, :, 0:qrows] / l_ref[1])
    o_ref[...] = (_rms(o_t.T, sw_ref[...]) * out_scale).astype(BF16)


def _diff_attn(qa, qb, k, vt, bias_tiles, far, lam, subln_w, batch, out_scale):
    t, d = k.shape
    seq = t // batch
    nheads = d // HEAD
    tile = ATTN_TILE
    width = ATTN_GROUP * tile
    qrows = ATTN_QTILES * tile
    nq = seq // qrows
    smem = pl.BlockSpec(memory_space=pltpu.SMEM)
    head_spec = pl.BlockSpec((seq, HEAD), lambda b, h, i: (b, h))
    return pl.pallas_call(
        functools.partial(_attn_kernel, out_scale=out_scale),
        grid=(batch, nheads, nq),
        in_specs=[smem, smem, head_spec, head_spec, head_spec,
                  pl.BlockSpec((1, seq // width, HEAD, width), lambda b, h, i: (b * nheads + h, 0, 0, 0)),
                  pl.BlockSpec((1, 2, 4, tile, tile), lambda b, h, i: (h, 0, 0, 0, 0)),
                  pl.BlockSpec((1, HEAD), lambda b, h, i: (0, 0))],
        out_specs=pl.BlockSpec((qrows, HEAD), lambda b, h, i: (b * nq + i, h)),
        out_shape=jax.ShapeDtypeStruct((t, d), BF16),
        scratch_shapes=[pltpu.VMEM((2, 1, qrows), F32)] * 3
        + [pltpu.VMEM((2, HEAD, qrows + ATTN_PAD), F32), pltpu.VMEM((2, width, qrows + ATTN_PAD), F32),
           pltpu.VMEM((2, width, qrows + ATTN_PAD), BF16)],
        compiler_params=_params("parallel", "parallel", "arbitrary"),
        name="diff_attn",
    )(lam.reshape(1).astype(F32), far, qa, qb, k, vt, bias_tiles, subln_w.reshape(1, HEAD).astype(F32))


def kernel(x, a_norm_pre, a_norm_post, a_w_in, a_lb, a_gate_norm, a_w_out, kv_norm, w_kv,
           b_norm_pre, b_norm_post, b_w_q, b_lambda, b_subln, b_w_out, rel_bias,
           mlp_norm_pre, mlp_norm_post, mlp_w_up, mlp_w_down):
    batch, seq, d = x.shape
    n_a = a_w_in.shape[0]
    n_b = b_w_q.shape[0]
    nheads = d // HEAD
    bf = lambda w: w.astype(BF16)

    lb_all = jnp.cumsum(jax.nn.softmax(a_lb.astype(F32), axis=0), axis=0)
    lb_all = lb_all - lb_all[0:1]

    h = x.reshape(batch * seq, d)
    for a in range(n_a):
        q, b, k, v, sg = _hgrn_in(h, a_norm_pre[a], bf(a_w_in[a]), lb_all[a])
        o = _hgrn_rec(q, b, k, v, sg, a_gate_norm[a], batch)
        h = _proj_mlp(o, bf(a_w_out[a]), h, a_norm_post[a],
                      mlp_norm_pre[a], bf(mlp_w_up[a]), bf(mlp_w_down[a]), mlp_norm_post[a])

    k_sh, v_sh = _norm_mm(h, kv_norm, bf(w_kv))
    width = ATTN_GROUP * ATTN_TILE
    vt = v_sh.reshape(batch, seq // width, width, nheads, HEAD).transpose(0, 3, 1, 4, 2)
    vt = vt.reshape(batch * nheads, seq // width, HEAD, width)
    bias_tiles, far = _bias_tables(rel_bias, nheads)
    q_scale = (HEAD // 2) ** -0.5 * LOG2E
    for bi in range(n_b):
        layer = n_a + bi
        lam_init = 0.8 - 0.6 * math.exp(-0.3 * layer)
        lp = b_lambda[bi].astype(F32)
        lam = jnp.exp(jnp.sum(lp[0] * lp[1])) - jnp.exp(jnp.sum(lp[2] * lp[3])) + lam_init
        qa, qb = _norm_mm(h, b_norm_pre[bi], bf(b_w_q[bi]), scale=q_scale, split_maps=True)
        o = _diff_attn(qa, qb, k_sh, vt, bias_tiles, far, lam, b_subln[bi], batch, 1.0 - lam_init)
        h = _proj_mlp(o, bf(b_w_out[bi]), h, b_norm_post[bi],
                      mlp_norm_pre[layer], bf(mlp_w_up[layer]), bf(mlp_w_down[layer]), mlp_norm_post[layer])
    return h.reshape(batch, seq, d)
```

```python
import functools
import math

import jax
import jax.numpy as jnp
from jax import lax
from jax.experimental import pallas as pl
from jax.experimental.pallas import tpu as pltpu

EPS = 1e-6
NEG_INF = -1e30
CHUNK = 64
HEAD = 128
REL_BUCKETS = 32
REL_MAX_DIST = 128
LOG2E = math.log2(math.e)

ROW_TILE = 1024
REC_ROWS = 1024
REC_SUB = 16
REC_UNROLL = 4
ATTN_TILE = 256
ATTN_ROWS = 64
ATTN_UNROLL = 2
ATTN_QTILES = 2
ATTN_GROUP = 2
BIAS_NEAR, BIAS_DIAG, BIAS_FAR, BIAS_MASKED = range(4)
VMEM_LIMIT = 56 * 1024 * 1024

F32 = jnp.float32
BF16 = jnp.bfloat16


def _rms(x, w):
    return x * lax.rsqrt(jnp.mean(x * x, axis=-1, keepdims=True) + EPS) * w


def _sigmoid(x):
    return 1.0 / (1.0 + jnp.exp(-x))


def _params(*sem):
    return pltpu.CompilerParams(dimension_semantics=sem, vmem_limit_bytes=VMEM_LIMIT)


def _resident(shape):
    return pl.BlockSpec(shape, lambda *_: (0,) * len(shape), pipeline_mode=pl.Buffered(1))


def _hgrn_in_kernel(h_ref, nw_ref, w_ref, loglb_ref, log1mlb_ref, onemlb_ref,
                    q_ref, b_ref, k_ref, v_ref, sg_ref):
    d = h_ref.shape[1]
    xn = _rms(h_ref[...], nw_ref[...]).astype(BF16)

    def proj(c):
        return jnp.dot(xn, w_ref[:, c * d:(c + 1) * d], preferred_element_type=F32)

    q = proj(0)
    q_ref[...] = (q * _sigmoid(q)).astype(BF16)
    f = proj(1)
    e = jnp.exp(-jnp.abs(f))
    inv = 1.0 / (1.0 + e)
    log_sig = jnp.minimum(f, 0.0) + jnp.log(inv)
    a = loglb_ref[...]
    c = log1mlb_ref[...] + log_sig
    log_f = jnp.maximum(a, c) + jnp.log(1.0 + jnp.exp(-jnp.abs(a - c)))
    b = log_f * LOG2E
    row_in_blk = lax.broadcasted_iota(jnp.int32, b.shape, 0) % REC_SUB
    shift = 1
    while shift < REC_SUB:
        b = b + jnp.where(row_in_blk >= shift, pltpu.roll(b, shift, axis=0), 0.0)
        shift *= 2
    b_ref[...] = b
    k_ref[...] = (onemlb_ref[...] * jnp.where(f > 0.0, e * inv, inv)).astype(BF16)
    v_ref[...] = proj(2).astype(BF16)
    g = proj(3)
    sg_ref[...] = (g * _sigmoid(g)).astype(BF16)


def _hgrn_in(h, nw, w, lb):
    t, d = h.shape
    row = lambda i: (i, 0)
    vec = lambda a: a.reshape(1, d).astype(F32)
    out_bf = jax.ShapeDtypeStruct((t, d), BF16)
    return pl.pallas_call(
        _hgrn_in_kernel,
        grid=(t // ROW_TILE,),
        in_specs=[pl.BlockSpec((ROW_TILE, d), row), _resident((1, d)), _resident(w.shape),
                  _resident((1, d)), _resident((1, d)), _resident((1, d))],
        out_specs=[pl.BlockSpec((ROW_TILE, d), row)] * 5,
        out_shape=[out_bf, jax.ShapeDtypeStruct((t, d), F32), out_bf, out_bf, out_bf],
        compiler_params=_params("parallel"),
        name="hgrn_in",
    )(h, vec(nw), w, vec(jnp.log(lb)), vec(jnp.log1p(-lb)), vec(1.0 - lb))


def _hgrn_rec_kernel(q_ref, b_ref, k_ref, v_ref, sg_ref, gnw_ref, o_ref, st_ref, *, nheads):
    sub = REC_SUB
    half = sub // 2
    nblk = REC_UNROLL
    chunk = nblk * sub

    @pl.when(pl.program_id(1) == 0)
    def _():
        st_ref[...] = jnp.zeros_like(st_ref)

    col = lax.broadcasted_iota(jnp.int32, (half, sub), 1)
    causal = (lax.broadcasted_iota(jnp.int32, (sub, sub), 1)
              <= lax.broadcasted_iota(jnp.int32, (sub, sub), 0))
    gnw = gnw_ref[...]
    heads = range(nheads)
    blocks = range(nblk)
    cs = [slice(h * HEAD, (h + 1) * HEAD) for h in heads]
    nt = (((1,), (1,)), ((), ()))
    tn = (((0,), (0,)), ((), ()))

    def body(trip, carry):
        base = pl.multiple_of(trip * chunk, chunk)
        rows_all = pl.ds(base, chunk)
        rows = [pl.ds(pl.multiple_of(base + j * sub, sub), sub) for j in blocks]
        b = [[b_ref[rows[j], cs[h]] for h in heads] for j in blocks]
        q = [[q_ref[rows[j], cs[h]].astype(F32) for h in heads] for j in blocks]

        inter = [None] * nheads
        cross = [[None] * nheads for _ in blocks]
        for h in heads:
            tot = [b[j][h][sub - 1:sub, :] for j in blocks]
            start = [None] * (nblk + 1)
            for j in range(1, nblk + 1):
                start[j] = tot[j - 1] if start[j - 1] is None else start[j - 1] + tot[j - 1]
            qe = [q[j][h] * jnp.exp2(b[j][h]) for j in blocks]
            kd = [k_ref[rows[j], cs[h]].astype(F32) * jnp.exp2(tot[j] - b[j][h]) for j in blocks]
            q_chunk = [qe[j] if start[j] is None else qe[j] * jnp.exp2(start[j]) for j in blocks]
            k_chunk = [kd[j] if j == nblk - 1 else kd[j] * jnp.exp2(start[nblk] - start[j + 1]) for j in blocks]
            inter[h] = lax.dot_general(jnp.concatenate(q_chunk, axis=0).astype(BF16), st_ref[h].astype(BF16),
                                       nt, preferred_element_type=F32)
            upd = lax.dot_general(v_ref[rows_all, cs[h]], jnp.concatenate(k_chunk, axis=0).astype(BF16),
                                  tn, preferred_element_type=F32)
            st_ref[h] = st_ref[h] * jnp.exp2(start[nblk]) + upd
            for i in range(1, nblk):
                keys = [kd[j] if j == i - 1 else kd[j] * jnp.exp2(start[i] - start[j + 1]) for j in range(i)]
                cross[i][h] = lax.dot_general(qe[i].astype(BF16), jnp.concatenate(keys, axis=0).astype(BF16),
                                              nt, preferred_element_type=F32)

        for j in blocks:
            raw = [None] * nheads
            half_cross = [None] * nheads
            for h in heads:
                lo = slice(0, half)
                hi = slice(half, sub)
                pieces = []
                for s in range(sub):
                    rows_s = lo if s < half else hi
                    b_s = b[j][h][s:s + 1, :]
                    pieces.append(q[j][h][rows_s] * jnp.exp2(jnp.minimum(b[j][h][rows_s] - b_s, 0.0)))
                stacked = jnp.concatenate(pieces, axis=0).astype(BF16)
                raw[h] = lax.dot_general(stacked, k_ref[rows[j], cs[h]], nt, preferred_element_type=F32)
                b_mid = b[j][h][half - 1:half, :]
                q_x = jnp.concatenate([jnp.zeros((half, HEAD), F32), q[j][h][hi] * jnp.exp2(b[j][h][hi] - b_mid)], axis=0)
                k_x = jnp.concatenate([k_ref[rows[j], cs[h]].astype(F32)[lo] * jnp.exp2(b_mid - b[j][h][lo]),
                                       jnp.zeros((half, HEAD), F32)], axis=0)
                half_cross[h] = lax.dot_general(q_x.astype(BF16), k_x.astype(BF16), nt, preferred_element_type=F32)

            for h in heads:
                s_lo = jnp.zeros((half, sub), F32)
                s_hi = jnp.zeros((half, sub), F32)
                for s in range(sub):
                    blk = raw[h][s * half:(s + 1) * half]
                    if s < half:
                        s_lo = jnp.where(col == s, blk, s_lo)
                    else:
                        s_hi = jnp.where(col == s, blk, s_hi)
                scores = (jnp.where(causal, jnp.concatenate([s_lo, s_hi], axis=0), 0.0) + half_cross[h]).astype(BF16)
                o = inter[h][j * sub:(j + 1) * sub] + jnp.dot(scores, v_ref[rows[j], cs[h]],
                                                              preferred_element_type=F32)
                if j > 0:
                    o = o + jnp.dot(cross[j][h].astype(BF16), v_ref[pl.ds(base, j * sub), cs[h]],
                                    preferred_element_type=F32)
                og = _rms(o, gnw) * sg_ref[rows[j], cs[h]].astype(F32)
                o_ref[rows[j], cs[h]] = og.astype(BF16)
        return carry

    lax.fori_loop(0, q_ref.shape[0] // chunk, body, 0)


def _hgrn_rec(q, b, k, v, sg, gnw, batch):
    t, d = q.shape
    nheads = d // HEAD
    nt = t // batch // REC_ROWS
    row = lambda b, i: (b * nt + i, 0)
    blk = pl.BlockSpec((REC_ROWS, d), row)
    return pl.pallas_call(
        functools.partial(_hgrn_rec_kernel, nheads=nheads),
        grid=(batch, nt),
        in_specs=[blk, blk, blk, blk, blk, _resident((1, HEAD))],
        out_specs=blk,
        out_shape=jax.ShapeDtypeStruct((t, d), BF16),
        scratch_shapes=[pltpu.VMEM((nheads, HEAD, HEAD), F32)],
        compiler_params=_params("parallel", "arbitrary"),
        name="hgrn_rec",
    )(q, b, k, v, sg, gnw.reshape(1, HEAD).astype(F32))


def _proj_mlp_kernel(x_ref, wo_ref, h_ref, npost_ref, n1_ref, wu_ref, wd_ref, n2_ref, o_ref, u_ref):
    mix = jnp.dot(x_ref[...], wo_ref[...], preferred_element_type=F32)
    h = h_ref[...] + _rms(mix, npost_ref[...])
    d = h.shape[1]
    xn = _rms(h, n1_ref[...]).astype(BF16)
    for c in range(wu_ref.shape[1] // d):
        u = jnp.maximum(jnp.dot(xn, wu_ref[:, c * d:(c + 1) * d], preferred_element_type=F32), 0.0)
        u_ref[:, c * d:(c + 1) * d] = (u * u).astype(BF16)
    y = jnp.dot(u_ref[...], wd_ref[...], preferred_element_type=F32)
    o_ref[...] = h + _rms(y, n2_ref[...])


def _proj_mlp(x, wo, h, npost, n1, wu, wd, n2):
    t, d = h.shape
    row = lambda i: (i, 0)
    vec = lambda a: a.reshape(1, d).astype(F32)
    blk = pl.BlockSpec((ROW_TILE, d), row)
    return pl.pallas_call(
        _proj_mlp_kernel,
        grid=(t // ROW_TILE,),
        in_specs=[blk, _resident(wo.shape), blk, _resident((1, d)), _resident((1, d)), _resident(wu.shape),
                  _resident(wd.shape), _resident((1, d))],
        out_specs=blk,
        out_shape=jax.ShapeDtypeStruct((t, d), F32),
        scratch_shapes=[pltpu.VMEM((ROW_TILE, wu.shape[1]), BF16)],
        compiler_params=_params("parallel"),
        name="proj_mlp",
    )(x, wo, h, vec(npost), vec(n1), wu, wd, vec(n2))


def _norm_mm_kernel(h_ref, nw_ref, w_ref, *o_refs, scale, split_maps):
    d = h_ref.shape[1]
    xn = _rms(h_ref[...], nw_ref[...]).astype(BF16)
    if split_maps:
        y = (jnp.dot(xn, w_ref[...], preferred_element_type=F32) * scale).astype(BF16)
        first = lax.broadcasted_iota(jnp.int32, y.shape, 1) % HEAD < HEAD // 2
        zero = jnp.zeros_like(y)
        o_refs[0][...] = jnp.where(first, y, zero)
        o_refs[1][...] = jnp.where(first, zero, y)
        return
    for c, o_ref in enumerate(o_refs):
        y = jnp.dot(xn, w_ref[:, c * d:(c + 1) * d], preferred_element_type=F32)
        o_ref[...] = (y * scale).astype(BF16)


def _norm_mm(h, nw, w, scale=1.0, split_maps=False):
    t, d = h.shape
    n_out = 2 if split_maps else w.shape[1] // d
    row = lambda i: (i, 0)
    return pl.pallas_call(
        functools.partial(_norm_mm_kernel, scale=scale, split_maps=split_maps),
        grid=(t // ROW_TILE,),
        in_specs=[pl.BlockSpec((ROW_TILE, d), row), _resident((1, d)), _resident(w.shape)],
        out_specs=[pl.BlockSpec((ROW_TILE, d), row)] * n_out,
        out_shape=[jax.ShapeDtypeStruct((t, d), BF16)] * n_out,
        compiler_params=_params("parallel"),
        name="norm_mm",
    )(h, nw.reshape(1, d).astype(F32), w)


def _rel_bucket(rel):
    half = REL_BUCKETS // 2
    max_exact = half // 2
    ret = jnp.where(rel > 0, half, 0)
    n = jnp.abs(rel)
    nf = jnp.maximum(n, 1).astype(jnp.float32)
    large = max_exact + (jnp.log(nf / max_exact) / math.log(REL_MAX_DIST / max_exact)
                         * (half - max_exact)).astype(jnp.int32)
    large = jnp.minimum(large, half - 1)
    return ret + jnp.where(n < max_exact, n, large)


def _bias_tables(rel_bias, nheads):
    tile = ATTN_TILE
    kp = jnp.arange(tile)[:, None]
    qp = jnp.arange(tile)[None, :]
    table = rel_bias.astype(F32) * LOG2E

    def lookup(rel):
        onehot = (_rel_bucket(rel)[..., None] == jnp.arange(REL_BUCKETS)).astype(F32)
        b = jnp.dot(onehot, table, precision=lax.Precision.HIGHEST)
        return b.reshape(tile, tile, nheads, 2).transpose(2, 3, 0, 1)

    far = table[_rel_bucket(jnp.int32(-(tile + 1)))]
    kinds = [None] * 4
    kinds[BIAS_NEAR] = lookup(kp - tile - qp)
    kinds[BIAS_DIAG] = jnp.where(kp // CHUNK <= qp // CHUNK, lookup(kp - qp), NEG_INF)
    kinds[BIAS_FAR] = jnp.broadcast_to(far.reshape(nheads, 2, 1, 1), (nheads, 2, tile, tile))
    kinds[BIAS_MASKED] = jnp.full((nheads, 2, tile, tile), NEG_INF, F32)
    return jnp.stack(kinds, axis=2), far


def _attn_kernel(lam_ref, far_ref, qa_ref, qb_ref, k_ref, vt_ref, bias_ref, sw_ref, o_ref,
                 m_ref, l_ref, alpha_ref, acc_ref, s_ref, p_ref, *, out_scale):
    h = pl.program_id(1)
    qi = pl.program_id(2)
    i0 = qi * ATTN_QTILES
    group = ATTN_GROUP
    tile = ATTN_TILE
    width = group * tile
    qrows = ATTN_QTILES * tile
    nsteps = (i0 + ATTN_QTILES - 1) // group + 1

    def logits(t, q_index):
        start = pl.multiple_of(t * width, width)
        kblk = k_ref[pl.ds(start, width), :]
        q_start = pl.multiple_of(q_index * qrows, qrows)
        for m, q_ref in enumerate((qa_ref, qb_ref)):
            s_ref[m] = lax.dot_general(kblk, q_ref[pl.ds(q_start, qrows), :], (((1,), (1,)), ((), ())),
                                       preferred_element_type=F32)

    def tile_kind(j, i):
        return jnp.where(j <= i - 2, BIAS_FAR,
                         jnp.where(j == i - 1, BIAS_NEAR, jnp.where(j == i, BIAS_DIAG, BIAS_MASKED)))

    fold = lambda x, op: op(x.reshape(x.shape[0] // 8, 8, x.shape[1]), axis=0)
    pieces = range(width // ATTN_ROWS)

    def softmax(t, edge):
        for m in range(2):
            if edge:
                s = s_ref[m]
                s = jnp.concatenate(
                    [jnp.concatenate(
                        [s[g * tile:(g + 1) * tile, u * tile:(u + 1) * tile]
                         + bias_ref[0, m, tile_kind(t * group + g, i0 + u)] for u in range(ATTN_QTILES)], axis=1)
                     for g in range(group)], axis=0)
                m_cur = jnp.max(s, axis=0, keepdims=True)
            else:
                far_c = far_ref[2 * h + m]
                col_max = None
                for c in pieces:
                    part = fold(s_ref[m, c * ATTN_ROWS:(c + 1) * ATTN_ROWS, :], jnp.max)
                    col_max = part if col_max is None else jnp.maximum(col_max, part)
                m_cur = jnp.max(col_max, axis=0, keepdims=True) + far_c
            m_prev = m_ref[m]
            m_new = jnp.maximum(m_prev, m_cur)
            alpha = jnp.exp2(m_prev - m_new)
            if edge:
                p = jnp.exp2(s - m_new)
                p_ref[m] = p.astype(BF16)
                p_sum = jnp.sum(p, axis=0, keepdims=True)
            else:
                col_sum = None
                for c in pieces:
                    rows = slice(c * ATTN_ROWS, (c + 1) * ATTN_ROWS)
                    p = jnp.exp2(s_ref[m, rows, :] - (m_new - far_c))
                    p_ref[m, rows, :] = p.astype(BF16)
                    part = fold(p, jnp.sum)
                    col_sum = part if col_sum is None else col_sum + part
                p_sum = jnp.sum(col_sum, axis=0, keepdims=True)
            l_ref[m] = alpha * l_ref[m] + p_sum
            alpha_ref[m] = alpha
            m_ref[m] = m_new

    def values(t):
        vtblk = vt_ref[0, jnp.maximum(t, 0)]
        for m in range(2):
            acc_ref[m] = alpha_ref[m] * acc_ref[m] + jnp.dot(vtblk, p_ref[m], preferred_element_type=F32)

    @pl.when(qi == 0)
    def _():
        l_ref[...] = jnp.zeros_like(l_ref)
        acc_ref[...] = jnp.zeros_like(acc_ref)
        p_ref[...] = jnp.zeros_like(p_ref)
        logits(0, qi)

    m_ref[...] = jnp.full_like(m_ref, NEG_INF)
    alpha_ref[...] = jnp.ones_like(alpha_ref)

    def far_step(t):
        values(t - 1)
        softmax(t, edge=False)
        logits(t + 1, qi)

    def far_group(u, carry):
        for j in range(ATTN_UNROLL):
            far_step(ATTN_UNROLL * u + j)
        return carry

    def far_single(t, carry):
        far_step(t)
        return carry

    nfar = jnp.maximum(nsteps - 2, 0)
    ngroups = nfar // ATTN_UNROLL
    lax.fori_loop(0, ngroups, far_group, 0)
    lax.fori_loop(ngroups * ATTN_UNROLL, nfar, far_single, 0)

    @pl.when(nsteps >= 2)
    def _():
        values(nsteps - 3)
        softmax(nsteps - 2, edge=True)
        logits(nsteps - 1, qi)

    values(nsteps - 2)
    softmax(nsteps - 1, edge=True)
    values(nsteps - 1)
    logits(0, jnp.minimum(qi + 1, pl.num_programs(2) - 1))

    o_t = acc_ref[0] / l_ref[0] - lam_ref[0] * (acc_ref[1] / l_ref[1])
    o_ref[...] = (_rms(o_t.T, sw_ref[...]) * out_scale).astype(BF16)


def _diff_attn(qa, qb, k, vt, bias_tiles, far, lam, subln_w, batch, out_scale):
    t, d = k.shape
    seq = t // batch
    nheads = d // HEAD
    tile = ATTN_TILE
    width = ATTN_GROUP * tile
    qrows = ATTN_QTILES * tile
    nq = seq // qrows
    smem = pl.BlockSpec(memory_space=pltpu.SMEM)
    head_spec = pl.BlockSpec((seq, HEAD), lambda b, h, i: (b, h))
    return pl.pallas_call(
        functools.partial(_attn_kernel, out_scale=out_scale),
        grid=(batch, nheads, nq),
        in_specs=[smem, smem, head_spec, head_spec, head_spec,
                  pl.BlockSpec((1, seq // width, HEAD, width), lambda b, h, i: (b * nheads + h, 0, 0, 0)),
                  pl.BlockSpec((1, 2, 4, tile, tile), lambda b, h, i: (h, 0, 0, 0, 0)),
                  pl.BlockSpec((1, HEAD), lambda b, h, i: (0, 0))],
        out_specs=pl.BlockSpec((qrows, HEAD), lambda b, h, i: (b * nq + i, h)),
        out_shape=jax.ShapeDtypeStruct((t, d), BF16),
        scratch_shapes=[pltpu.VMEM((2, 1, qrows), F32)] * 3
        + [pltpu.VMEM((2, HEAD, qrows), F32), pltpu.VMEM((2, width, qrows), F32),
           pltpu.VMEM((2, width, qrows), BF16)],
        compiler_params=_params("parallel", "parallel", "arbitrary"),
        name="diff_attn",
    )(lam.reshape(1).astype(F32), far, qa, qb, k, vt, bias_tiles, subln_w.reshape(1, HEAD).astype(F32))


def kernel(x, a_norm_pre, a_norm_post, a_w_in, a_lb, a_gate_norm, a_w_out, kv_norm, w_kv,
           b_norm_pre, b_norm_post, b_w_q, b_lambda, b_subln, b_w_out, rel_bias,
           mlp_norm_pre, mlp_norm_post, mlp_w_up, mlp_w_down):
    batch, seq, d = x.shape
    n_a = a_w_in.shape[0]
    n_b = b_w_q.shape[0]
    nheads = d // HEAD
    bf = lambda w: w.astype(BF16)

    lb_all = jnp.cumsum(jax.nn.softmax(a_lb.astype(F32), axis=0), axis=0)
    lb_all = lb_all - lb_all[0:1]

    h = x.reshape(batch * seq, d)
    for a in range(n_a):
        q, b, k, v, sg = _hgrn_in(h, a_norm_pre[a], bf(a_w_in[a]), lb_all[a])
        o = _hgrn_rec(q, b, k, v, sg, a_gate_norm[a], batch)
        h = _proj_mlp(o, bf(a_w_out[a]), h, a_norm_post[a],
                      mlp_norm_pre[a], bf(mlp_w_up[a]), bf(mlp_w_down[a]), mlp_norm_post[a])

    k_sh, v_sh = _norm_mm(h, kv_norm, bf(w_kv))
    width = ATTN_GROUP * ATTN_TILE
    vt = v_sh.reshape(batch, seq // width, width, nheads, HEAD).transpose(0, 3, 1, 4, 2)
    vt = vt.reshape(batch * nheads, seq // width, HEAD, width)
    bias_tiles, far = _bias_tables(rel_bias, nheads)
    q_scale = (HEAD // 2) ** -0.5 * LOG2E
    for bi in range(n_b):
        layer = n_a + bi
        lam_init = 0.8 - 0.6 * math.exp(-0.3 * layer)
        lp = b_lambda[bi].astype(F32)
        lam = jnp.exp(jnp.sum(lp[0] * lp[1])) - jnp.exp(jnp.sum(lp[2] * lp[3])) + lam_init
        qa, qb = _norm_mm(h, b_norm_pre[bi], bf(b_w_q[bi]), scale=q_scale, split_maps=True)
        o = _diff_attn(qa, qb, k_sh, vt, bias_tiles, far, lam, b_subln[bi], batch, 1.0 - lam_init)
        h = _proj_mlp(o, bf(b_w_out[bi]), h, b_norm_post[bi],
                      mlp_norm_pre[layer], bf(mlp_w_up[layer]), bf(mlp_w_down[layer]), mlp_norm_post[layer])
    return h.reshape(batch, seq, d)
```

```python
import functools
import math

import jax
import jax.numpy as jnp
from jax import lax
from jax.experimental import pallas as pl
from jax.experimental.pallas import tpu as pltpu

EPS = 1e-6
NEG_INF = -1e30
CHUNK = 64
HEAD = 128
REL_BUCKETS = 32
REL_MAX_DIST = 128
LOG2E = math.log2(math.e)

ROW_TILE = 1024
REC_ROWS = 1024
REC_SUB = 16
REC_UNROLL = 4
ATTN_TILE = 256
ATTN_ROWS = 64
ATTN_UNROLL = 2
ATTN_QTILES = 2
ATTN_GROUP = 2
BIAS_NEAR, BIAS_DIAG, BIAS_FAR, BIAS_MASKED = range(4)
VMEM_LIMIT = 56 * 1024 * 1024

F32 = jnp.float32
BF16 = jnp.bfloat16


def _rms(x, w):
    return x * lax.rsqrt(jnp.mean(x * x, axis=-1, keepdims=True) + EPS) * w


def _sigmoid(x):
    return 1.0 / (1.0 + jnp.exp(-x))


def _params(*sem):
    return pltpu.CompilerParams(dimension_semantics=sem, vmem_limit_bytes=VMEM_LIMIT)


def _resident(shape):
    return pl.BlockSpec(shape, lambda *_: (0,) * len(shape), pipeline_mode=pl.Buffered(1))


def _hgrn_in_kernel(h_ref, nw_ref, w_ref, loglb_ref, log1mlb_ref, onemlb_ref,
                    q_ref, b_ref, k_ref, v_ref, sg_ref):
    d = h_ref.shape[1]
    xn = _rms(h_ref[...], nw_ref[...]).astype(BF16)

    def proj(c):
        return jnp.dot(xn, w_ref[:, c * d:(c + 1) * d], preferred_element_type=F32)

    q = proj(0)
    q_ref[...] = (q * _sigmoid(q)).astype(BF16)
    f = proj(1)
    e = jnp.exp(-jnp.abs(f))
    inv = 1.0 / (1.0 + e)
    log_sig = jnp.minimum(f, 0.0) + jnp.log(inv)
    a = loglb_ref[...]
    c = log1mlb_ref[...] + log_sig
    log_f = jnp.maximum(a, c) + jnp.log(1.0 + jnp.exp(-jnp.abs(a - c)))
    b = log_f * LOG2E
    row_in_blk = lax.broadcasted_iota(jnp.int32, b.shape, 0) % REC_SUB
    shift = 1
    while shift < REC_SUB:
        b = b + jnp.where(row_in_blk >= shift, pltpu.roll(b, shift, axis=0), 0.0)
        shift *= 2
    b_ref[...] = b
    k_ref[...] = (onemlb_ref[...] * jnp.where(f > 0.0, e * inv, inv)).astype(BF16)
    v_ref[...] = proj(2).astype(BF16)
    g = proj(3)
    sg_ref[...] = (g * _sigmoid(g)).astype(BF16)


def _hgrn_in(h, nw, w, lb):
    t, d = h.shape
    row = lambda i: (i, 0)
    vec = lambda a: a.reshape(1, d).astype(F32)
    out_bf = jax.ShapeDtypeStruct((t, d), BF16)
    return pl.pallas_call(
        _hgrn_in_kernel,
        grid=(t // ROW_TILE,),
        in_specs=[pl.BlockSpec((ROW_TILE, d), row), _resident((1, d)), _resident(w.shape),
                  _resident((1, d)), _resident((1, d)), _resident((1, d))],
        out_specs=[pl.BlockSpec((ROW_TILE, d), row)] * 5,
        out_shape=[out_bf, jax.ShapeDtypeStruct((t, d), F32), out_bf, out_bf, out_bf],
        compiler_params=_params("parallel"),
        name="hgrn_in",
    )(h, vec(nw), w, vec(jnp.log(lb)), vec(jnp.log1p(-lb)), vec(1.0 - lb))


def _hgrn_rec_kernel(q_ref, b_ref, k_ref, v_ref, sg_ref, gnw_ref, o_ref,
                     st_ref, inter_ref, cross_ref, raw_ref, mid_ref, *, nheads):
    sub = REC_SUB
    half = sub // 2
    nblk = REC_UNROLL
    chunk = nblk * sub
    nchunks = q_ref.shape[0] // chunk

    @pl.when(pl.program_id(1) == 0)
    def _():
        st_ref[...] = jnp.zeros_like(st_ref)

    col = lax.broadcasted_iota(jnp.int32, (half, sub), 1)
    causal = (lax.broadcasted_iota(jnp.int32, (sub, sub), 1)
              <= lax.broadcasted_iota(jnp.int32, (sub, sub), 0))
    gnw = gnw_ref[...]
    heads = range(nheads)
    blocks = range(nblk)
    cs = [slice(h * HEAD, (h + 1) * HEAD) for h in heads]
    nt = (((1,), (1,)), ((), ()))
    tn = (((0,), (0,)), ((), ()))

    def block_rows(c, j):
        return pl.ds(pl.multiple_of(c * chunk + j * sub, sub), sub)

    def products(c):
        base = pl.multiple_of(c * chunk, chunk)
        rows = [block_rows(c, j) for j in blocks]
        b = [[b_ref[rows[j], cs[h]] for h in heads] for j in blocks]
        q = [[q_ref[rows[j], cs[h]].astype(F32) for h in heads] for j in blocks]

        for h in heads:
            tot = [b[j][h][sub - 1:sub, :] for j in blocks]
            start = [None] * (nblk + 1)
            for j in range(1, nblk + 1):
                start[j] = tot[j - 1] if start[j - 1] is None else start[j - 1] + tot[j - 1]
            qe = [q[j][h] * jnp.exp2(b[j][h]) for j in blocks]
            kd = [k_ref[rows[j], cs[h]].astype(F32) * jnp.exp2(tot[j] - b[j][h]) for j in blocks]
            q_chunk = [qe[j] if start[j] is None else qe[j] * jnp.exp2(start[j]) for j in blocks]
            k_chunk = [kd[j] if j == nblk - 1 else kd[j] * jnp.exp2(start[nblk] - start[j + 1]) for j in blocks]
            inter_ref[h] = lax.dot_general(jnp.concatenate(q_chunk, axis=0).astype(BF16),
                                           st_ref[h].astype(BF16), nt, preferred_element_type=F32)
            upd = lax.dot_general(v_ref[pl.ds(base, chunk), cs[h]], jnp.concatenate(k_chunk, axis=0).astype(BF16),
                                  tn, preferred_element_type=F32)
            st_ref[h] = st_ref[h] * jnp.exp2(start[nblk]) + upd
            for i in range(1, nblk):
                keys = [kd[j] if j == i - 1 else kd[j] * jnp.exp2(start[i] - start[j + 1]) for j in range(i)]
                cross_ref[i, h, :, 0:i * sub] = lax.dot_general(
                    qe[i].astype(BF16), jnp.concatenate(keys, axis=0).astype(BF16), nt,
                    preferred_element_type=F32)

        for j in blocks:
            for h in heads:
                lo = slice(0, half)
                hi = slice(half, sub)
                pieces = []
                for s in range(sub):
                    rows_s = lo if s < half else hi
                    b_s = b[j][h][s:s + 1, :]
                    pieces.append(q[j][h][rows_s] * jnp.exp2(jnp.minimum(b[j][h][rows_s] - b_s, 0.0)))
                stacked = jnp.concatenate(pieces, axis=0).astype(BF16)
                raw_ref[j, h] = lax.dot_general(stacked, k_ref[rows[j], cs[h]], nt, preferred_element_type=F32)
                b_mid = b[j][h][half - 1:half, :]
                q_x = jnp.concatenate([jnp.zeros((half, HEAD), F32), q[j][h][hi] * jnp.exp2(b[j][h][hi] - b_mid)], axis=0)
                k_x = jnp.concatenate([k_ref[rows[j], cs[h]].astype(F32)[lo] * jnp.exp2(b_mid - b[j][h][lo]),
                                       jnp.zeros((half, HEAD), F32)], axis=0)
                mid_ref[j, h] = lax.dot_general(q_x.astype(BF16), k_x.astype(BF16), nt, preferred_element_type=F32)

    def outputs(c):
        base = pl.multiple_of(c * chunk, chunk)
        for j in blocks:
            rows = block_rows(c, j)
            for h in heads:
                s_lo = jnp.zeros((half, sub), F32)
                s_hi = jnp.zeros((half, sub), F32)
                for s in range(sub):
                    blk = raw_ref[j, h, s * half:(s + 1) * half, :]
                    if s < half:
                        s_lo = jnp.where(col == s, blk, s_lo)
                    else:
                        s_hi = jnp.where(col == s, blk, s_hi)
                scores = (jnp.where(causal, jnp.concatenate([s_lo, s_hi], axis=0), 0.0) + mid_ref[j, h]).astype(BF16)
                o = inter_ref[h, j * sub:(j + 1) * sub, :] + jnp.dot(scores, v_ref[rows, cs[h]],
                                                                      preferred_element_type=F32)
                if j > 0:
                    o = o + jnp.dot(cross_ref[j, h, :, 0:j * sub].astype(BF16), v_ref[pl.ds(base, j * sub), cs[h]],
                                    preferred_element_type=F32)
                og = _rms(o, gnw) * sg_ref[rows, cs[h]].astype(F32)
                o_ref[rows, cs[h]] = og.astype(BF16)

    def body(c, carry):
        outputs(c - 1)
        products(c)
        return carry

    products(0)
    lax.fori_loop(1, nchunks, body, 0)
    outputs(nchunks - 1)


def _hgrn_rec(q, b, k, v, sg, gnw, batch):
    t, d = q.shape
    nheads = d // HEAD
    nt = t // batch // REC_ROWS
    row = lambda b, i: (b * nt + i, 0)
    blk = pl.BlockSpec((REC_ROWS, d), row)
    return pl.pallas_call(
        functools.partial(_hgrn_rec_kernel, nheads=nheads),
        grid=(batch, nt),
        in_specs=[blk, blk, blk, blk, blk, _resident((1, HEAD))],
        out_specs=blk,
        out_shape=jax.ShapeDtypeStruct((t, d), BF16),
        scratch_shapes=[pltpu.VMEM((nheads, HEAD, HEAD), F32),
                        pltpu.VMEM((nheads, REC_UNROLL * REC_SUB, HEAD), F32),
                        pltpu.VMEM((REC_UNROLL, nheads, REC_SUB, REC_UNROLL * REC_SUB), F32),
                        pltpu.VMEM((REC_UNROLL, nheads, REC_SUB * REC_SUB // 2, REC_SUB), F32),
                        pltpu.VMEM((REC_UNROLL, nheads, REC_SUB, REC_SUB), F32)],
        compiler_params=_params("parallel", "arbitrary"),
        name="hgrn_rec",
    )(q, b, k, v, sg, gnw.reshape(1, HEAD).astype(F32))


def _proj_mlp_kernel(x_ref, wo_ref, h_ref, npost_ref, n1_ref, wu_ref, wd_ref, n2_ref, o_ref, u_ref):
    mix = jnp.dot(x_ref[...], wo_ref[...], preferred_element_type=F32)
    h = h_ref[...] + _rms(mix, npost_ref[...])
    d = h.shape[1]
    xn = _rms(h, n1_ref[...]).astype(BF16)
    for c in range(wu_ref.shape[1] // d):
        u = jnp.maximum(jnp.dot(xn, wu_ref[:, c * d:(c + 1) * d], preferred_element_type=F32), 0.0)
        u_ref[:, c * d:(c + 1) * d] = (u * u).astype(BF16)
    y = jnp.dot(u_ref[...], wd_ref[...], preferred_element_type=F32)
    o_ref[...] = h + _rms(y, n2_ref[...])


def _proj_mlp(x, wo, h, npost, n1, wu, wd, n2):
    t, d = h.shape
    row = lambda i: (i, 0)
    vec = lambda a: a.reshape(1, d).astype(F32)
    blk = pl.BlockSpec((ROW_TILE, d), row)
    return pl.pallas_call(
        _proj_mlp_kernel,
        grid=(t // ROW_TILE,),
        in_specs=[blk, _resident(wo.shape), blk, _resident((1, d)), _resident((1, d)), _resident(wu.shape),
                  _resident(wd.shape), _resident((1, d))],
        out_specs=blk,
        out_shape=jax.ShapeDtypeStruct((t, d), F32),
        scratch_shapes=[pltpu.VMEM((ROW_TILE, wu.shape[1]), BF16)],
        compiler_params=_params("parallel"),
        name="proj_mlp",
    )(x, wo, h, vec(npost), vec(n1), wu, wd, vec(n2))


def _norm_mm_kernel(h_ref, nw_ref, w_ref, *o_refs, scale, split_maps):
    d = h_ref.shape[1]
    xn = _rms(h_ref[...], nw_ref[...]).astype(BF16)
    if split_maps:
        y = (jnp.dot(xn, w_ref[...], preferred_element_type=F32) * scale).astype(BF16)
        first = lax.broadcasted_iota(jnp.int32, y.shape, 1) % HEAD < HEAD // 2
        zero = jnp.zeros_like(y)
        o_refs[0][...] = jnp.where(first, y, zero)
        o_refs[1][...] = jnp.where(first, zero, y)
        return
    for c, o_ref in enumerate(o_refs):
        y = jnp.dot(xn, w_ref[:, c * d:(c + 1) * d], preferred_element_type=F32)
        o_ref[...] = (y * scale).astype(BF16)


def _norm_mm(h, nw, w, scale=1.0, split_maps=False):
    t, d = h.shape
    n_out = 2 if split_maps else w.shape[1] // d
    row = lambda i: (i, 0)
    return pl.pallas_call(
        functools.partial(_norm_mm_kernel, scale=scale, split_maps=split_maps),
        grid=(t // ROW_TILE,),
        in_specs=[pl.BlockSpec((ROW_TILE, d), row), _resident((1, d)), _resident(w.shape)],
        out_specs=[pl.BlockSpec((ROW_TILE, d), row)] * n_out,
        out_shape=[jax.ShapeDtypeStruct((t, d), BF16)] * n_out,
        compiler_params=_params("parallel"),
        name="norm_mm",
    )(h, nw.reshape(1, d).astype(F32), w)


def _rel_bucket(rel):
    half = REL_BUCKETS // 2
    max_exact = half // 2
    ret = jnp.where(rel > 0, half, 0)
    n = jnp.abs(rel)
    nf = jnp.maximum(n, 1).astype(jnp.float32)
    large = max_exact + (jnp.log(nf / max_exact) / math.log(REL_MAX_DIST / max_exact)
                         * (half - max_exact)).astype(jnp.int32)
    large = jnp.minimum(large, half - 1)
    return ret + jnp.where(n < max_exact, n, large)


def _bias_tables(rel_bias, nheads):
    tile = ATTN_TILE
    kp = jnp.arange(tile)[:, None]
    qp = jnp.arange(tile)[None, :]
    table = rel_bias.astype(F32) * LOG2E

    def lookup(rel):
        onehot = (_rel_bucket(rel)[..., None] == jnp.arange(REL_BUCKETS)).astype(F32)
        b = jnp.dot(onehot, table, precision=lax.Precision.HIGHEST)
        return b.reshape(tile, tile, nheads, 2).transpose(2, 3, 0, 1)

    far = table[_rel_bucket(jnp.int32(-(tile + 1)))]
    kinds = [None] * 4
    kinds[BIAS_NEAR] = lookup(kp - tile - qp)
    kinds[BIAS_DIAG] = jnp.where(kp // CHUNK <= qp // CHUNK, lookup(kp - qp), NEG_INF)
    kinds[BIAS_FAR] = jnp.broadcast_to(far.reshape(nheads, 2, 1, 1), (nheads, 2, tile, tile))
    kinds[BIAS_MASKED] = jnp.full((nheads, 2, tile, tile), NEG_INF, F32)
    return jnp.stack(kinds, axis=2), far


def _attn_kernel(lam_ref, far_ref, qa_ref, qb_ref, k_ref, vt_ref, bias_ref, sw_ref, o_ref,
                 m_ref, l_ref, alpha_ref, acc_ref, s_ref, p_ref, *, out_scale):
    h = pl.program_id(1)
    qi = pl.program_id(2)
    i0 = qi * ATTN_QTILES
    group = ATTN_GROUP
    tile = ATTN_TILE
    width = group * tile
    qrows = ATTN_QTILES * tile
    nsteps = (i0 + ATTN_QTILES - 1) // group + 1

    def logits(t, q_index):
        start = pl.multiple_of(t * width, width)
        kblk = k_ref[pl.ds(start, width), :]
        q_start = pl.multiple_of(q_index * qrows, qrows)
        for m, q_ref in enumerate((qa_ref, qb_ref)):
            s_ref[m] = lax.dot_general(kblk, q_ref[pl.ds(q_start, qrows), :], (((1,), (1,)), ((), ())),
                                       preferred_element_type=F32)

    def tile_kind(j, i):
        return jnp.where(j <= i - 2, BIAS_FAR,
                         jnp.where(j == i - 1, BIAS_NEAR, jnp.where(j == i, BIAS_DIAG, BIAS_MASKED)))

    fold = lambda x, op: op(x.reshape(x.shape[0] // 8, 8, x.shape[1]), axis=0)
    pieces = range(width // ATTN_ROWS)

    def softmax(t, edge):
        for m in range(2):
            if edge:
                s = s_ref[m]
                s = jnp.concatenate(
                    [jnp.concatenate(
                        [s[g * tile:(g + 1) * tile, u * tile:(u + 1) * tile]
                         + bias_ref[0, m, tile_kind(t * group + g, i0 + u)] for u in range(ATTN_QTILES)], axis=1)
                     for g in range(group)], axis=0)
                m_cur = jnp.max(s, axis=0, keepdims=True)
            else:
                far_c = far_ref[2 * h + m]
                col_max = None
                for c in pieces:
                    part = fold(s_ref[m, c * ATTN_ROWS:(c + 1) * ATTN_ROWS, :], jnp.max)
                    col_max = part if col_max is None else jnp.maximum(col_max, part)
                m_cur = jnp.max(col_max, axis=0, keepdims=True) + far_c
            m_prev = m_ref[m]
            m_new = jnp.maximum(m_prev, m_cur)
            alpha = jnp.exp2(m_prev - m_new)
            if edge:
                p = jnp.exp2(s - m_new)
                p_ref[m] = p.astype(BF16)
                p_sum = jnp.sum(p, axis=0, keepdims=True)
            else:
                col_sum = None
                for c in pieces:
                    rows = slice(c * ATTN_ROWS, (c + 1) * ATTN_ROWS)
                    p = jnp.exp2(s_ref[m, rows, :] - (m_new - far_c))
                    p_ref[m, rows, :] = p.astype(BF16)
                    part = fold(p, jnp.sum)
                    col_sum = part if col_sum is None else col_sum + part
                p_sum = jnp.sum(col_sum, axis=0, keepdims=True)
            l_ref[m] = alpha * l_ref[m] + p_sum
            alpha_ref[m] = alpha
            m_ref[m] = m_new

    def values(t):
        vtblk = vt_ref[0, jnp.maximum(t, 0)]
        for m in range(2):
            acc_ref[m] = alpha_ref[m] * acc_ref[m] + jnp.dot(vtblk, p_ref[m], preferred_element_type=F32)

    @pl.when(qi == 0)
    def _():
        l_ref[...] = jnp.zeros_like(l_ref)
        acc_ref[...] = jnp.zeros_like(acc_ref)
        p_ref[...] = jnp.zeros_like(p_ref)
        logits(0, qi)

    m_ref[...] = jnp.full_like(m_ref, NEG_INF)
    alpha_ref[...] = jnp.ones_like(alpha_ref)

    def far_step(t):
        values(t - 1)
        softmax(t, edge=False)
        logits(t + 1, qi)

    def far_group(u, carry):
        for j in range(ATTN_UNROLL):
            far_step(ATTN_UNROLL * u + j)
        return carry

    def far_single(t, carry):
        far_step(t)
        return carry

    nfar = jnp.maximum(nsteps - 2, 0)
    ngroups = nfar // ATTN_UNROLL
    lax.fori_loop(0, ngroups, far_group, 0)
    lax.fori_loop(ngroups * ATTN_UNROLL, nfar, far_single, 0)

    @pl.when(nsteps >= 2)
    def _():
        values(nsteps - 3)
        softmax(nsteps - 2, edge=True)
        logits(nsteps - 1, qi)

    values(nsteps - 2)
    softmax(nsteps - 1, edge=True)
    values(nsteps - 1)
    logits(0, jnp.minimum(qi + 1, pl.num_programs(2) - 1))

    o_t = acc_ref[0] / l_ref[0] - lam_ref[0] * (acc_ref[1] / l_ref[1])
    o_ref[...] = (_rms(o_t.T, sw_ref[...]) * out_scale).astype(BF16)


def _diff_attn(qa, qb, k, vt, bias_tiles, far, lam, subln_w, batch, out_scale):
    t, d = k.shape
    seq = t // batch
    nheads = d // HEAD
    tile = ATTN_TILE
    width = ATTN_GROUP * tile
    qrows = ATTN_QTILES * tile
    nq = seq // qrows
    smem = pl.BlockSpec(memory_space=pltpu.SMEM)
    head_spec = pl.BlockSpec((seq, HEAD), lambda b, h, i: (b, h))
    return pl.pallas_call(
        functools.partial(_attn_kernel, out_scale=out_scale),
        grid=(batch, nheads, nq),
        in_specs=[smem, smem, head_spec, head_spec, head_spec,
                  pl.BlockSpec((1, seq // width, HEAD, width), lambda b, h, i: (b * nheads + h, 0, 0, 0)),
                  pl.BlockSpec((1, 2, 4, tile, tile), lambda b, h, i: (h, 0, 0, 0, 0)),
                  pl.BlockSpec((1, HEAD), lambda b, h, i: (0, 0))],
        out_specs=pl.BlockSpec((qrows, HEAD), lambda b, h, i: (b * nq + i, h)),
        out_shape=jax.ShapeDtypeStruct((t, d), BF16),
        scratch_shapes=[pltpu.VMEM((2, 1, qrows), F32)] * 3
        + [pltpu.VMEM((2, HEAD, qrows), F32), pltpu.VMEM((2, width, qrows), F32),
           pltpu.VMEM((2, width, qrows), BF16)],
        compiler_params=_params("parallel", "parallel", "arbitrary"),
        name="diff_attn",
    )(lam.reshape(1).astype(F32), far, qa, qb, k, vt, bias_tiles, subln_w.reshape(1, HEAD).astype(F32))


def kernel(x, a_norm_pre, a_norm_post, a_w_in, a_lb, a_gate_norm, a_w_out, kv_norm, w_kv,
           b_norm_pre, b_norm_post, b_w_q, b_lambda, b_subln, b_w_out, rel_bias,
           mlp_norm_pre, mlp_norm_post, mlp_w_up, mlp_w_down):
    batch, seq, d = x.shape
    n_a = a_w_in.shape[0]
    n_b = b_w_q.shape[0]
    nheads = d // HEAD
    bf = lambda w: w.astype(BF16)

    lb_all = jnp.cumsum(jax.nn.softmax(a_lb.astype(F32), axis=0), axis=0)
    lb_all = lb_all - lb_all[0:1]

    h = x.reshape(batch * seq, d)
    for a in range(n_a):
        q, b, k, v, sg = _hgrn_in(h, a_norm_pre[a], bf(a_w_in[a]), lb_all[a])
        o = _hgrn_rec(q, b, k, v, sg, a_gate_norm[a], batch)
        h = _proj_mlp(o, bf(a_w_out[a]), h, a_norm_post[a],
                      mlp_norm_pre[a], bf(mlp_w_up[a]), bf(mlp_w_down[a]), mlp_norm_post[a])

    k_sh, v_sh = _norm_mm(h, kv_norm, bf(w_kv))
    width = ATTN_GROUP * ATTN_TILE
    vt = v_sh.reshape(batch, seq // width, width, nheads, HEAD).transpose(0, 3, 1, 4, 2)
    vt = vt.reshape(batch * nheads, seq // width, HEAD, width)
    bias_tiles, far = _bias_tables(rel_bias, nheads)
    q_scale = (HEAD // 2) ** -0.5 * LOG2E
    for bi in range(n_b):
        layer = n_a + bi
        lam_init = 0.8 - 0.6 * math.exp(-0.3 * layer)
        lp = b_lambda[bi].astype(F32)
        lam = jnp.exp(jnp.sum(lp[0] * lp[1])) - jnp.exp(jnp.sum(lp[2] * lp[3])) + lam_init
        qa, qb = _norm_mm(h, b_norm_pre[bi], bf(b_w_q[bi]), scale=q_scale, split_maps=True)
        o = _diff_attn(qa, qb, k_sh, vt, bias_tiles, far, lam, b_subln[bi], batch, 1.0 - lam_init)
        h = _proj_mlp(o, bf(b_w_out[bi]), h, b_norm_post[bi],
                      mlp_norm_pre[layer], bf(mlp_w_up[layer]), bf(mlp_w_down[layer]), mlp_norm_post[layer])
    return h.reshape(batch, seq, d)
```

```python
import functools
import math

import jax
import jax.numpy as jnp
from jax import lax
from jax.experimental import pallas as pl
from jax.experimental.pallas import tpu as pltpu

EPS = 1e-6
NEG_INF = -1e30
CHUNK = 64
HEAD = 128
REL_BUCKETS = 32
REL_MAX_DIST = 128
LOG2E = math.log2(math.e)

ROW_TILE = 1024
REC_ROWS = 1024
REC_SUB = 16
REC_UNROLL = 4
ATTN_TILE = 256
ATTN_ROWS = 64
ATTN_UNROLL = 2
ATTN_PAD = 128
ATTN_QTILES = 2
ATTN_GROUP = 2
BIAS_NEAR, BIAS_DIAG, BIAS_FAR, BIAS_MASKED = range(4)
VMEM_LIMIT = 56 * 1024 * 1024

F32 = jnp.float32
BF16 = jnp.bfloat16


def _rms(x, w):
    return x * lax.rsqrt(jnp.mean(x * x, axis=-1, keepdims=True) + EPS) * w


def _sigmoid(x):
    return 1.0 / (1.0 + jnp.exp(-x))


def _params(*sem):
    return pltpu.CompilerParams(dimension_semantics=sem, vmem_limit_bytes=VMEM_LIMIT)


def _resident(shape):
    return pl.BlockSpec(shape, lambda *_: (0,) * len(shape), pipeline_mode=pl.Buffered(1))


def _hgrn_in_kernel(h_ref, nw_ref, w_ref, loglb_ref, log1mlb_ref, onemlb_ref,
                    q_ref, b_ref, k_ref, v_ref, sg_ref):
    d = h_ref.shape[1]
    xn = _rms(h_ref[...], nw_ref[...]).astype(BF16)

    def proj(c):
        return jnp.dot(xn, w_ref[:, c * d:(c + 1) * d], preferred_element_type=F32)

    q = proj(0)
    q_ref[...] = (q * _sigmoid(q)).astype(BF16)
    f = proj(1)
    e = jnp.exp(-jnp.abs(f))
    inv = 1.0 / (1.0 + e)
    log_sig = jnp.minimum(f, 0.0) + jnp.log(inv)
    a = loglb_ref[...]
    c = log1mlb_ref[...] + log_sig
    log_f = jnp.maximum(a, c) + jnp.log(1.0 + jnp.exp(-jnp.abs(a - c)))
    b = log_f * LOG2E
    row_in_blk = lax.broadcasted_iota(jnp.int32, b.shape, 0) % REC_SUB
    shift = 1
    while shift < REC_SUB:
        b = b + jnp.where(row_in_blk >= shift, pltpu.roll(b, shift, axis=0), 0.0)
        shift *= 2
    b_ref[...] = b
    k_ref[...] = (onemlb_ref[...] * jnp.where(f > 0.0, e * inv, inv)).astype(BF16)
    v_ref[...] = proj(2).astype(BF16)
    g = proj(3)
    sg_ref[...] = (g * _sigmoid(g)).astype(BF16)


def _hgrn_in(h, nw, w, lb):
    t, d = h.shape
    row = lambda i: (i, 0)
    vec = lambda a: a.reshape(1, d).astype(F32)
    out_bf = jax.ShapeDtypeStruct((t, d), BF16)
    return pl.pallas_call(
        _hgrn_in_kernel,
        grid=(t // ROW_TILE,),
        in_specs=[pl.BlockSpec((ROW_TILE, d), row), _resident((1, d)), _resident(w.shape),
                  _resident((1, d)), _resident((1, d)), _resident((1, d))],
        out_specs=[pl.BlockSpec((ROW_TILE, d), row)] * 5,
        out_shape=[out_bf, jax.ShapeDtypeStruct((t, d), F32), out_bf, out_bf, out_bf],
        compiler_params=_params("parallel"),
        name="hgrn_in",
    )(h, vec(nw), w, vec(jnp.log(lb)), vec(jnp.log1p(-lb)), vec(1.0 - lb))


def _hgrn_rec_kernel(q_ref, b_ref, k_ref, v_ref, sg_ref, gnw_ref, o_ref,
                     st_ref, inter_ref, cross_ref, raw_ref, mid_ref, row_ref, *, nheads):
    sub = REC_SUB
    half = sub // 2
    nblk = REC_UNROLL
    chunk = nblk * sub
    nchunks = q_ref.shape[0] // chunk

    @pl.when(pl.program_id(1) == 0)
    def _():
        st_ref[...] = jnp.zeros_like(st_ref)

    col = lax.broadcasted_iota(jnp.int32, (half, sub), 1)
    causal = (lax.broadcasted_iota(jnp.int32, (sub, sub), 1)
              <= lax.broadcasted_iota(jnp.int32, (sub, sub), 0))
    gnw = gnw_ref[...]
    heads = range(nheads)
    blocks = range(nblk)
    cs = [slice(h * HEAD, (h + 1) * HEAD) for h in heads]
    nt = (((1,), (1,)), ((), ()))
    tn = (((0,), (0,)), ((), ()))

    def block_rows(c, j):
        return pl.ds(pl.multiple_of(c * chunk + j * sub, sub), sub)

    def products(c):
        base = pl.multiple_of(c * chunk, chunk)
        rows = [block_rows(c, j) for j in blocks]
        b = [[b_ref[rows[j], cs[h]] for h in heads] for j in blocks]
        q = [[q_ref[rows[j], cs[h]].astype(F32) for h in heads] for j in blocks]

        for h in heads:
            tot = [b[j][h][sub - 1:sub, :] for j in blocks]
            start = [None] * (nblk + 1)
            for j in range(1, nblk + 1):
                start[j] = tot[j - 1] if start[j - 1] is None else start[j - 1] + tot[j - 1]
            qe = [q[j][h] * jnp.exp2(b[j][h]) for j in blocks]
            kd = [k_ref[rows[j], cs[h]].astype(F32) * jnp.exp2(tot[j] - b[j][h]) for j in blocks]
            q_chunk = [qe[j] if start[j] is None else qe[j] * jnp.exp2(start[j]) for j in blocks]
            k_chunk = [kd[j] if j == nblk - 1 else kd[j] * jnp.exp2(start[nblk] - start[j + 1]) for j in blocks]
            inter_ref[h] = lax.dot_general(jnp.concatenate(q_chunk, axis=0).astype(BF16),
                                           st_ref[h].astype(BF16), nt, preferred_element_type=F32)
            upd = lax.dot_general(v_ref[pl.ds(base, chunk), cs[h]], jnp.concatenate(k_chunk, axis=0).astype(BF16),
                                  tn, preferred_element_type=F32)
            st_ref[h] = st_ref[h] * jnp.exp2(start[nblk]) + upd
            for i in range(1, nblk):
                keys = [kd[j] if j == i - 1 else kd[j] * jnp.exp2(start[i] - start[j + 1]) for j in range(i)]
                cross_ref[i, h, :, 0:i * sub] = lax.dot_general(
                    qe[i].astype(BF16), jnp.concatenate(keys, axis=0).astype(BF16), nt,
                    preferred_element_type=F32)

        for j in blocks:
            for h in heads:
                lo = slice(0, half)
                hi = slice(half, sub)
                row_ref[j, h] = b[j][h]
                pieces = []
                for s in range(sub):
                    rows_s = lo if s < half else hi
                    b_s = row_ref[j, h, s:s + 1, :]
                    pieces.append(q[j][h][rows_s] * jnp.exp2(jnp.minimum(b[j][h][rows_s] - b_s, 0.0)))
                stacked = jnp.concatenate(pieces, axis=0).astype(BF16)
                raw_ref[j, h] = lax.dot_general(stacked, k_ref[rows[j], cs[h]], nt, preferred_element_type=F32)
                b_mid = b[j][h][half - 1:half, :]
                q_x = jnp.concatenate([jnp.zeros((half, HEAD), F32), q[j][h][hi] * jnp.exp2(b[j][h][hi] - b_mid)], axis=0)
                k_x = jnp.concatenate([k_ref[rows[j], cs[h]].astype(F32)[lo] * jnp.exp2(b_mid - b[j][h][lo]),
                                       jnp.zeros((half, HEAD), F32)], axis=0)
                mid_ref[j, h] = lax.dot_general(q_x.astype(BF16), k_x.astype(BF16), nt, preferred_element_type=F32)

    def outputs(c):
        base = pl.multiple_of(c * chunk, chunk)
        for j in blocks:
            rows = block_rows(c, j)
            for h in heads:
                s_lo = jnp.zeros((half, sub), F32)
                s_hi = jnp.zeros((half, sub), F32)
                for s in range(sub):
                    blk = raw_ref[j, h, s * half:(s + 1) * half, :]
                    if s < half:
                        s_lo = jnp.where(col == s, blk, s_lo)
                    else:
                        s_hi = jnp.where(col == s, blk, s_hi)
                scores = (jnp.where(causal, jnp.concatenate([s_lo, s_hi], axis=0), 0.0) + mid_ref[j, h]).astype(BF16)
                o = inter_ref[h, j * sub:(j + 1) * sub, :] + jnp.dot(scores, v_ref[rows, cs[h]],
                                                                      preferred_element_type=F32)
                if j > 0:
                    o = o + jnp.dot(cross_ref[j, h, :, 0:j * sub].astype(BF16), v_ref[pl.ds(base, j * sub), cs[h]],
                                    preferred_element_type=F32)
                og = _rms(o, gnw) * sg_ref[rows, cs[h]].astype(F32)
                o_ref[rows, cs[h]] = og.astype(BF16)

    def body(c, carry):
        outputs(c - 1)
        products(c)
        return carry

    products(0)
    lax.fori_loop(1, nchunks, body, 0)
    outputs(nchunks - 1)


def _hgrn_rec(q, b, k, v, sg, gnw, batch):
    t, d = q.shape
    nheads = d // HEAD
    nt = t // batch // REC_ROWS
    row = lambda b, i: (b * nt + i, 0)
    blk = pl.BlockSpec((REC_ROWS, d), row)
    return pl.pallas_call(
        functools.partial(_hgrn_rec_kernel, nheads=nheads),
        grid=(batch, nt),
        in_specs=[blk, blk, blk, blk, blk, _resident((1, HEAD))],
        out_specs=blk,
        out_shape=jax.ShapeDtypeStruct((t, d), BF16),
        scratch_shapes=[pltpu.VMEM((nheads, HEAD, HEAD), F32),
                        pltpu.VMEM((nheads, REC_UNROLL * REC_SUB, HEAD), F32),
                        pltpu.VMEM((REC_UNROLL, nheads, REC_SUB, REC_UNROLL * REC_SUB), F32),
                        pltpu.VMEM((REC_UNROLL, nheads, REC_SUB * REC_SUB // 2, REC_SUB), F32),
                        pltpu.VMEM((REC_UNROLL, nheads, REC_SUB, REC_SUB), F32),
                        pltpu.VMEM((REC_UNROLL, nheads, REC_SUB, HEAD), F32)],
        compiler_params=_params("parallel", "arbitrary"),
        name="hgrn_rec",
    )(q, b, k, v, sg, gnw.reshape(1, HEAD).astype(F32))


def _proj_mlp_kernel(x_ref, wo_ref, h_ref, npost_ref, n1_ref, wu_ref, wd_ref, n2_ref, o_ref, u_ref):
    mix = jnp.dot(x_ref[...], wo_ref[...], preferred_element_type=F32)
    h = h_ref[...] + _rms(mix, npost_ref[...])
    d = h.shape[1]
    xn = _rms(h, n1_ref[...]).astype(BF16)
    for c in range(wu_ref.shape[1] // d):
        u = jnp.maximum(jnp.dot(xn, wu_ref[:, c * d:(c + 1) * d], preferred_element_type=F32), 0.0)
        u_ref[:, c * d:(c + 1) * d] = (u * u).astype(BF16)
    y = jnp.dot(u_ref[...], wd_ref[...], preferred_element_type=F32)
    o_ref[...] = h + _rms(y, n2_ref[...])


def _proj_mlp(x, wo, h, npost, n1, wu, wd, n2):
    t, d = h.shape
    row = lambda i: (i, 0)
    vec = lambda a: a.reshape(1, d).astype(F32)
    blk = pl.BlockSpec((ROW_TILE, d), row)
    return pl.pallas_call(
        _proj_mlp_kernel,
        grid=(t // ROW_TILE,),
        in_specs=[blk, _resident(wo.shape), blk, _resident((1, d)), _resident((1, d)), _resident(wu.shape),
                  _resident(wd.shape), _resident((1, d))],
        out_specs=blk,
        out_shape=jax.ShapeDtypeStruct((t, d), F32),
        scratch_shapes=[pltpu.VMEM((ROW_TILE, wu.shape[1]), BF16)],
        compiler_params=_params("parallel"),
        name="proj_mlp",
    )(x, wo, h, vec(npost), vec(n1), wu, wd, vec(n2))


def _norm_mm_kernel(h_ref, nw_ref, w_ref, *o_refs, scale, split_maps):
    d = h_ref.shape[1]
    xn = _rms(h_ref[...], nw_ref[...]).astype(BF16)
    if split_maps:
        y = (jnp.dot(xn, w_ref[...], preferred_element_type=F32) * scale).astype(BF16)
        first = lax.broadcasted_iota(jnp.int32, y.shape, 1) % HEAD < HEAD // 2
        zero = jnp.zeros_like(y)
        o_refs[0][...] = jnp.where(first, y, zero)
        o_refs[1][...] = jnp.where(first, zero, y)
        return
    for c, o_ref in enumerate(o_refs):
        y = jnp.dot(xn, w_ref[:, c * d:(c + 1) * d], preferred_element_type=F32)
        o_ref[...] = (y * scale).astype(BF16)


def _norm_mm(h, nw, w, scale=1.0, split_maps=False):
    t, d = h.shape
    n_out = 2 if split_maps else w.shape[1] // d
    row = lambda i: (i, 0)
    return pl.pallas_call(
        functools.partial(_norm_mm_kernel, scale=scale, split_maps=split_maps),
        grid=(t // ROW_TILE,),
        in_specs=[pl.BlockSpec((ROW_TILE, d), row), _resident((1, d)), _resident(w.shape)],
        out_specs=[pl.BlockSpec((ROW_TILE, d), row)] * n_out,
        out_shape=[jax.ShapeDtypeStruct((t, d), BF16)] * n_out,
        compiler_params=_params("parallel"),
        name="norm_mm",
    )(h, nw.reshape(1, d).astype(F32), w)


def _rel_bucket(rel):
    half = REL_BUCKETS // 2
    max_exact = half // 2
    ret = jnp.where(rel > 0, half, 0)
    n = jnp.abs(rel)
    nf = jnp.maximum(n, 1).astype(jnp.float32)
    large = max_exact + (jnp.log(nf / max_exact) / math.log(REL_MAX_DIST / max_exact)
                         * (half - max_exact)).astype(jnp.int32)
    large = jnp.minimum(large, half - 1)
    return ret + jnp.where(n < max_exact, n, large)


def _bias_tables(rel_bias, nheads):
    tile = ATTN_TILE
    kp = jnp.arange(tile)[:, None]
    qp = jnp.arange(tile)[None, :]
    table = rel_bias.astype(F32) * LOG2E

    def lookup(rel):
        onehot = (_rel_bucket(rel)[..., None] == jnp.arange(REL_BUCKETS)).astype(F32)
        b = jnp.dot(onehot, table, precision=lax.Precision.HIGHEST)
        return b.reshape(tile, tile, nheads, 2).transpose(2, 3, 0, 1)

    far = table[_rel_bucket(jnp.int32(-(tile + 1)))]
    kinds = [None] * 4
    kinds[BIAS_NEAR] = lookup(kp - tile - qp)
    kinds[BIAS_DIAG] = jnp.where(kp // CHUNK <= qp // CHUNK, lookup(kp - qp), NEG_INF)
    kinds[BIAS_FAR] = jnp.broadcast_to(far.reshape(nheads, 2, 1, 1), (nheads, 2, tile, tile))
    kinds[BIAS_MASKED] = jnp.full((nheads, 2, tile, tile), NEG_INF, F32)
    return jnp.stack(kinds, axis=2), far


def _attn_kernel(lam_ref, far_ref, qa_ref, qb_ref, k_ref, vt_ref, bias_ref, sw_ref, o_ref,
                 m_ref, l_ref, alpha_ref, acc_ref, s_ref, p_ref, *, out_scale):
    h = pl.program_id(1)
    qi = pl.program_id(2)
    i0 = qi * ATTN_QTILES
    group = ATTN_GROUP
    tile = ATTN_TILE
    width = group * tile
    qrows = ATTN_QTILES * tile
    nsteps = (i0 + ATTN_QTILES - 1) // group + 1

    def logits(t, q_index):
        start = pl.multiple_of(t * width, width)
        kblk = k_ref[pl.ds(start, width), :]
        q_start = pl.multiple_of(q_index * qrows, qrows)
        for m, q_ref in enumerate((qa_ref, qb_ref)):
            s_ref[m, :, 0:qrows] = lax.dot_general(kblk, q_ref[pl.ds(q_start, qrows), :], (((1,), (1,)), ((), ())),
                                                   preferred_element_type=F32)

    def tile_kind(j, i):
        return jnp.where(j <= i - 2, BIAS_FAR,
                         jnp.where(j == i - 1, BIAS_NEAR, jnp.where(j == i, BIAS_DIAG, BIAS_MASKED)))

    fold = lambda x, op: op(x.reshape(x.shape[0] // 8, 8, x.shape[1]), axis=0)
    pieces = range(width // ATTN_ROWS)

    def softmax(t, edge):
        for m in range(2):
            if edge:
                s = s_ref[m, :, 0:qrows]
                s = jnp.concatenate(
                    [jnp.concatenate(
                        [s[g * tile:(g + 1) * tile, u * tile:(u + 1) * tile]
                         + bias_ref[0, m, tile_kind(t * group + g, i0 + u)] for u in range(ATTN_QTILES)], axis=1)
                     for g in range(group)], axis=0)
                m_cur = jnp.max(s, axis=0, keepdims=True)
            else:
                far_c = far_ref[2 * h + m]
                col_max = None
                for c in pieces:
                    part = fold(s_ref[m, c * ATTN_ROWS:(c + 1) * ATTN_ROWS, 0:qrows], jnp.max)
                    col_max = part if col_max is None else jnp.maximum(col_max, part)
                m_cur = jnp.max(col_max, axis=0, keepdims=True) + far_c
            m_prev = m_ref[m]
            m_new = jnp.maximum(m_prev, m_cur)
            alpha = jnp.exp2(m_prev - m_new)
            if edge:
                p = jnp.exp2(s - m_new)
                p_ref[m, :, 0:qrows] = p.astype(BF16)
                p_sum = jnp.sum(p, axis=0, keepdims=True)
            else:
                col_sum = None
                for c in pieces:
                    rows = slice(c * ATTN_ROWS, (c + 1) * ATTN_ROWS)
                    p = jnp.exp2(s_ref[m, rows, 0:qrows] - (m_new - far_c))
                    p_ref[m, rows, 0:qrows] = p.astype(BF16)
                    part = fold(p, jnp.sum)
                    col_sum = part if col_sum is None else col_sum + part
                p_sum = jnp.sum(col_sum, axis=0, keepdims=True)
            l_ref[m] = alpha * l_ref[m] + p_sum
            alpha_ref[m] = alpha
            m_ref[m] = m_new

    def values(t):
        vtblk = vt_ref[0, jnp.maximum(t, 0)]
        for m in range(2):
            acc_ref[m, :, 0:qrows] = (alpha_ref[m] * acc_ref[m, :, 0:qrows]
                                      + jnp.dot(vtblk, p_ref[m, :, 0:qrows], preferred_element_type=F32))

    @pl.when(qi == 0)
    def _():
        l_ref[...] = jnp.zeros_like(l_ref)
        acc_ref[...] = jnp.zeros_like(acc_ref)
        p_ref[...] = jnp.zeros_like(p_ref)
        logits(0, qi)

    m_ref[...] = jnp.full_like(m_ref, NEG_INF)
    alpha_ref[...] = jnp.ones_like(alpha_ref)

    def far_step(t):
        values(t - 1)
        softmax(t, edge=False)
        logits(t + 1, qi)

    def far_group(u, carry):
        for j in range(ATTN_UNROLL):
            far_step(ATTN_UNROLL * u + j)
        return carry

    def far_single(t, carry):
        far_step(t)
        return carry

    nfar = jnp.maximum(nsteps - 2, 0)
    ngroups = nfar // ATTN_UNROLL
    lax.fori_loop(0, ngroups, far_group, 0)
    lax.fori_loop(ngroups * ATTN_UNROLL, nfar, far_single, 0)

    @pl.when(nsteps >= 2)
    def _():
        values(nsteps - 3)
        softmax(nsteps - 2, edge=True)
        logits(nsteps - 1, qi)

    values(nsteps - 2)
    softmax(nsteps - 1, edge=True)
    values(nsteps - 1)
    logits(0, jnp.minimum(qi + 1, pl.num_programs(2) - 1))

    o_t = acc_ref[0, :, 0:qrows] / l_ref[0] - lam_ref[0] * (acc_ref[1, :, 0:qrows] / l_ref[1])
    o_ref[...] = (_rms(o_t.T, sw_ref[...]) * out_scale).astype(BF16)


def _diff_attn(qa, qb, k, vt, bias_tiles, far, lam, subln_w, batch, out_scale):
    t, d = k.shape
    seq = t // batch
    nheads = d // HEAD
    tile = ATTN_TILE
    width = ATTN_GROUP * tile
    qrows = ATTN_QTILES * tile
    nq = seq // qrows
    smem = pl.BlockSpec(memory_space=pltpu.SMEM)
    head_spec = pl.BlockSpec((seq, HEAD), lambda b, h, i: (b, h))
    return pl.pallas_call(
        functools.partial(_attn_kernel, out_scale=out_scale),
        grid=(batch, nheads, nq),
        in_specs=[smem, smem, head_spec, head_spec, head_spec,
                  pl.BlockSpec((1, seq // width, HEAD, width), lambda b, h, i: (b * nheads + h, 0, 0, 0)),
                  pl.BlockSpec((1, 2, 4, tile, tile), lambda b, h, i: (h, 0, 0, 0, 0)),
                  pl.BlockSpec((1, HEAD), lambda b, h, i: (0, 0))],
        out_specs=pl.BlockSpec((qrows, HEAD), lambda b, h, i: (b * nq + i, h)),
        out_shape=jax.ShapeDtypeStruct((t, d), BF16),
        scratch_shapes=[pltpu.VMEM((2, 1, qrows), F32)] * 3
        + [pltpu.VMEM((2, HEAD, qrows + ATTN_PAD), F32), pltpu.VMEM((2, width, qrows + ATTN_PAD), F32),
           pltpu.VMEM((2, width, qrows + ATTN_PAD), BF16)],
        compiler_params=_params("parallel", "parallel", "arbitrary"),
        name="diff_attn",
    )(lam.reshape(1).astype(F32), far, qa, qb, k, vt, bias_tiles, subln_w.reshape(1, HEAD).astype(F32))


def kernel(x, a_norm_pre, a_norm_post, a_w_in, a_lb, a_gate_norm, a_w_out, kv_norm, w_kv,
           b_norm_pre, b_norm_post, b_w_q, b_lambda, b_subln, b_w_out, rel_bias,
           mlp_norm_pre, mlp_norm_post, mlp_w_up, mlp_w_down):
    batch, seq, d = x.shape
    n_a = a_w_in.shape[0]
    n_b = b_w_q.shape[0]
    nheads = d // HEAD
    bf = lambda w: w.astype(BF16)

    lb_all = jnp.cumsum(jax.nn.softmax(a_lb.astype(F32), axis=0), axis=0)
    lb_all = lb_all - lb_all[0:1]

    h = x.reshape(batch * seq, d)
    for a in range(n_a):
        q, b, k, v, sg = _hgrn_in(h, a_norm_pre[a], bf(a_w_in[a]), lb_all[a])
        o = _hgrn_rec(q, b, k, v, sg, a_gate_norm[a], batch)
        h = _proj_mlp(o, bf(a_w_out[a]), h, a_norm_post[a],
                      mlp_norm_pre[a], bf(mlp_w_up[a]), bf(mlp_w_down[a]), mlp_norm_post[a])

    k_sh, v_sh = _norm_mm(h, kv_norm, bf(w_kv))
    width = ATTN_GROUP * ATTN_TILE
    vt = v_sh.reshape(batch, seq // width, width, nheads, HEAD).transpose(0, 3, 1, 4, 2)
    vt = vt.reshape(batch * nheads, seq // width, HEAD, width)
    bias_tiles, far = _bias_tables(rel_bias, nheads)
    q_scale = (HEAD // 2) ** -0.5 * LOG2E
    for bi in range(n_b):
        layer = n_a + bi
        lam_init = 0.8 - 0.6 * math.exp(-0.3 * layer)
        lp = b_lambda[bi].astype(F32)
        lam = jnp.exp(jnp.sum(lp[0] * lp[1])) - jnp.exp(jnp.sum(lp[2] * lp[3])) + lam_init
        qa, qb = _norm_mm(h, b_norm_pre[bi], bf(b_w_q[bi]), scale=q_scale, split_maps=True)
        o = _diff_attn(qa, qb, k_sh, vt, bias_tiles, far, lam, b_subln[bi], batch, 1.0 - lam_init)
        h = _proj_mlp(o, bf(b_w_out[bi]), h, b_norm_post[bi],
                      mlp_norm_pre[layer], bf(mlp_w_up[layer]), bf(mlp_w_down[layer]), mlp_norm_post[layer])
    return h.reshape(batch, seq, d)
```

```python
import functools
import math

import jax
import jax.numpy as jnp
from jax import lax
from jax.experimental import pallas as pl
from jax.experimental.pallas import tpu as pltpu

EPS = 1e-6
NEG_INF = -1e30
CHUNK = 64
HEAD = 128
REL_BUCKETS = 32
REL_MAX_DIST = 128
LOG2E = math.log2(math.e)

ROW_TILE = 1024
REC_ROWS = 1024
REC_SUB = 16
REC_UNROLL = 4
ATTN_TILE = 256
ATTN_ROWS = 64
ATTN_UNROLL = 2
ATTN_PAD = 128
ATTN_QTILES = 2
ATTN_GROUP = 2
BIAS_NEAR, BIAS_DIAG, BIAS_FAR, BIAS_MASKED = range(4)
VMEM_LIMIT = 56 * 1024 * 1024

F32 = jnp.float32
BF16 = jnp.bfloat16


def _rms(x, w):
    return x * lax.rsqrt(jnp.mean(x * x, axis=-1, keepdims=True) + EPS) * w


def _sigmoid(x):
    return 1.0 / (1.0 + jnp.exp(-x))


def _params(*sem):
    return pltpu.CompilerParams(dimension_semantics=sem, vmem_limit_bytes=VMEM_LIMIT)


def _resident(shape):
    return pl.BlockSpec(shape, lambda *_: (0,) * len(shape), pipeline_mode=pl.Buffered(1))


def _layer_spec(stack, layer):
    return pl.BlockSpec((None,) + stack.shape[1:], lambda *_: (layer, 0, 0), pipeline_mode=pl.Buffered(1))


def _hgrn_in_kernel(h_ref, nw_ref, w_ref, loglb_ref, log1mlb_ref, onemlb_ref,
                    q_ref, b_ref, k_ref, v_ref, sg_ref):
    d = h_ref.shape[1]
    xn = _rms(h_ref[...], nw_ref[...]).astype(BF16)

    def proj(c):
        return jnp.dot(xn, w_ref[:, c * d:(c + 1) * d], preferred_element_type=F32)

    q = proj(0)
    q_ref[...] = (q * _sigmoid(q)).astype(BF16)
    f = proj(1)
    e = jnp.exp(-jnp.abs(f))
    inv = 1.0 / (1.0 + e)
    log_sig = jnp.minimum(f, 0.0) + jnp.log(inv)
    a = loglb_ref[...]
    c = log1mlb_ref[...] + log_sig
    log_f = jnp.maximum(a, c) + jnp.log(1.0 + jnp.exp(-jnp.abs(a - c)))
    b = log_f * LOG2E
    row_in_blk = lax.broadcasted_iota(jnp.int32, b.shape, 0) % REC_SUB
    shift = 1
    while shift < REC_SUB:
        b = b + jnp.where(row_in_blk >= shift, pltpu.roll(b, shift, axis=0), 0.0)
        shift *= 2
    b_ref[...] = b
    k_ref[...] = (onemlb_ref[...] * jnp.where(f > 0.0, e * inv, inv)).astype(BF16)
    v_ref[...] = proj(2).astype(BF16)
    g = proj(3)
    sg_ref[...] = (g * _sigmoid(g)).astype(BF16)


def _hgrn_in(h, nw, w, layer, lb):
    t, d = h.shape
    row = lambda i: (i, 0)
    vec = lambda a: a.reshape(1, d).astype(F32)
    out_bf = jax.ShapeDtypeStruct((t, d), BF16)
    return pl.pallas_call(
        _hgrn_in_kernel,
        grid=(t // ROW_TILE,),
        in_specs=[pl.BlockSpec((ROW_TILE, d), row), _resident((1, d)), _layer_spec(w, layer),
                  _resident((1, d)), _resident((1, d)), _resident((1, d))],
        out_specs=[pl.BlockSpec((ROW_TILE, d), row)] * 5,
        out_shape=[out_bf, jax.ShapeDtypeStruct((t, d), F32), out_bf, out_bf, out_bf],
        compiler_params=_params("parallel"),
        name="hgrn_in",
    )(h, vec(nw), w, vec(jnp.log(lb)), vec(jnp.log1p(-lb)), vec(1.0 - lb))


def _hgrn_rec_kernel(q_ref, b_ref, k_ref, v_ref, sg_ref, gnw_ref, o_ref,
                     st_ref, inter_ref, cross_ref, raw_ref, mid_ref, row_ref, *, nheads):
    sub = REC_SUB
    half = sub // 2
    nblk = REC_UNROLL
    chunk = nblk * sub
    nchunks = q_ref.shape[0] // chunk

    @pl.when(pl.program_id(1) == 0)
    def _():
        st_ref[...] = jnp.zeros_like(st_ref)

    col = lax.broadcasted_iota(jnp.int32, (half, sub), 1)
    causal = (lax.broadcasted_iota(jnp.int32, (sub, sub), 1)
              <= lax.broadcasted_iota(jnp.int32, (sub, sub), 0))
    gnw = gnw_ref[...]
    heads = range(nheads)
    blocks = range(nblk)
    cs = [slice(h * HEAD, (h + 1) * HEAD) for h in heads]
    nt = (((1,), (1,)), ((), ()))
    tn = (((0,), (0,)), ((), ()))

    def block_rows(c, j):
        return pl.ds(pl.multiple_of(c * chunk + j * sub, sub), sub)

    def products(c):
        base = pl.multiple_of(c * chunk, chunk)
        rows = [block_rows(c, j) for j in blocks]
        b = [[b_ref[rows[j], cs[h]] for h in heads] for j in blocks]
        q = [[q_ref[rows[j], cs[h]].astype(F32) for h in heads] for j in blocks]

        for h in heads:
            tot = [b[j][h][sub - 1:sub, :] for j in blocks]
            start = [None] * (nblk + 1)
            for j in range(1, nblk + 1):
                start[j] = tot[j - 1] if start[j - 1] is None else start[j - 1] + tot[j - 1]
            qe = [q[j][h] * jnp.exp2(b[j][h]) for j in blocks]
            kd = [k_ref[rows[j], cs[h]].astype(F32) * jnp.exp2(tot[j] - b[j][h]) for j in blocks]
            q_chunk = [qe[j] if start[j] is None else qe[j] * jnp.exp2(start[j]) for j in blocks]
            k_chunk = [kd[j] if j == nblk - 1 else kd[j] * jnp.exp2(start[nblk] - start[j + 1]) for j in blocks]
            inter_ref[h] = lax.dot_general(jnp.concatenate(q_chunk, axis=0).astype(BF16),
                                           st_ref[h].astype(BF16), nt, preferred_element_type=F32)
            upd = lax.dot_general(v_ref[pl.ds(base, chunk), cs[h]], jnp.concatenate(k_chunk, axis=0).astype(BF16),
                                  tn, preferred_element_type=F32)
            st_ref[h] = st_ref[h] * jnp.exp2(start[nblk]) + upd
            for i in range(1, nblk):
                keys = [kd[j] if j == i - 1 else kd[j] * jnp.exp2(start[i] - start[j + 1]) for j in range(i)]
                cross_ref[i, h, :, 0:i * sub] = lax.dot_general(
                    qe[i].astype(BF16), jnp.concatenate(keys, axis=0).astype(BF16), nt,
                    preferred_element_type=F32)

        for j in blocks:
            for h in heads:
                lo = slice(0, half)
                hi = slice(half, sub)
                row_ref[j, h] = b[j][h]
                pieces = []
                for s in range(sub):
                    rows_s = lo if s < half else hi
                    b_s = row_ref[j, h, s:s + 1, :]
                    pieces.append(q[j][h][rows_s] * jnp.exp2(jnp.minimum(b[j][h][rows_s] - b_s, 0.0)))
                stacked = jnp.concatenate(pieces, axis=0).astype(BF16)
                raw_ref[j, h] = lax.dot_general(stacked, k_ref[rows[j], cs[h]], nt, preferred_element_type=F32)
                b_mid = b[j][h][half - 1:half, :]
                q_x = jnp.concatenate([jnp.zeros((half, HEAD), F32), q[j][h][hi] * jnp.exp2(b[j][h][hi] - b_mid)], axis=0)
                k_x = jnp.concatenate([k_ref[rows[j], cs[h]].astype(F32)[lo] * jnp.exp2(b_mid - b[j][h][lo]),
                                       jnp.zeros((half, HEAD), F32)], axis=0)
                mid_ref[j, h] = lax.dot_general(q_x.astype(BF16), k_x.astype(BF16), nt, preferred_element_type=F32)

    def outputs(c):
        base = pl.multiple_of(c * chunk, chunk)
        for j in blocks:
            rows = block_rows(c, j)
            for h in heads:
                s_lo = jnp.zeros((half, sub), F32)
                s_hi = jnp.zeros((half, sub), F32)
                for s in range(sub):
                    blk = raw_ref[j, h, s * half:(s + 1) * half, :]
                    if s < half:
                        s_lo = jnp.where(col == s, blk, s_lo)
                    else:
                        s_hi = jnp.where(col == s, blk, s_hi)
                scores = (jnp.where(causal, jnp.concatenate([s_lo, s_hi], axis=0), 0.0) + mid_ref[j, h]).astype(BF16)
                o = inter_ref[h, j * sub:(j + 1) * sub, :] + jnp.dot(scores, v_ref[rows, cs[h]],
                                                                      preferred_element_type=F32)
                if j > 0:
                    o = o + jnp.dot(cross_ref[j, h, :, 0:j * sub].astype(BF16), v_ref[pl.ds(base, j * sub), cs[h]],
                                    preferred_element_type=F32)
                og = _rms(o, gnw) * sg_ref[rows, cs[h]].astype(F32)
                o_ref[rows, cs[h]] = og.astype(BF16)

    def body(c, carry):
        outputs(c - 1)
        products(c)
        return carry

    products(0)
    lax.fori_loop(1, nchunks, body, 0)
    outputs(nchunks - 1)


def _hgrn_rec(q, b, k, v, sg, gnw, batch):
    t, d = q.shape
    nheads = d // HEAD
    nt = t // batch // REC_ROWS
    row = lambda b, i: (b * nt + i, 0)
    blk = pl.BlockSpec((REC_ROWS, d), row)
    return pl.pallas_call(
        functools.partial(_hgrn_rec_kernel, nheads=nheads),
        grid=(batch, nt),
        in_specs=[blk, blk, blk, blk, blk, _resident((1, HEAD))],
        out_specs=blk,
        out_shape=jax.ShapeDtypeStruct((t, d), BF16),
        scratch_shapes=[pltpu.VMEM((nheads, HEAD, HEAD), F32),
                        pltpu.VMEM((nheads, REC_UNROLL * REC_SUB, HEAD), F32),
                        pltpu.VMEM((REC_UNROLL, nheads, REC_SUB, REC_UNROLL * REC_SUB), F32),
                        pltpu.VMEM((REC_UNROLL, nheads, REC_SUB * REC_SUB // 2, REC_SUB), F32),
                        pltpu.VMEM((REC_UNROLL, nheads, REC_SUB, REC_SUB), F32),
                        pltpu.VMEM((REC_UNROLL, nheads, REC_SUB, HEAD), F32)],
        compiler_params=_params("parallel", "arbitrary"),
        name="hgrn_rec",
    )(q, b, k, v, sg, gnw.reshape(1, HEAD).astype(F32))


def _proj_mlp_kernel(x_ref, wo_ref, h_ref, npost_ref, n1_ref, wu_ref, wd_ref, n2_ref, o_ref, u_ref):
    mix = jnp.dot(x_ref[...], wo_ref[...], preferred_element_type=F32)
    h = h_ref[...] + _rms(mix, npost_ref[...])
    d = h.shape[1]
    xn = _rms(h, n1_ref[...]).astype(BF16)
    for c in range(wu_ref.shape[1] // d):
        u = jnp.maximum(jnp.dot(xn, wu_ref[:, c * d:(c + 1) * d], preferred_element_type=F32), 0.0)
        u_ref[:, c * d:(c + 1) * d] = (u * u).astype(BF16)
    y = jnp.dot(u_ref[...], wd_ref[...], preferred_element_type=F32)
    o_ref[...] = h + _rms(y, n2_ref[...])


def _proj_mlp(x, wo, wo_layer, h, npost, n1, wu, wd, mlp_layer, n2):
    t, d = h.shape
    row = lambda i: (i, 0)
    vec = lambda a: a.reshape(1, d).astype(F32)
    blk = pl.BlockSpec((ROW_TILE, d), row)
    return pl.pallas_call(
        _proj_mlp_kernel,
        grid=(t // ROW_TILE,),
        in_specs=[blk, _layer_spec(wo, wo_layer), blk, _resident((1, d)), _resident((1, d)),
                  _layer_spec(wu, mlp_layer), _layer_spec(wd, mlp_layer), _resident((1, d))],
        out_specs=blk,
        out_shape=jax.ShapeDtypeStruct((t, d), F32),
        scratch_shapes=[pltpu.VMEM((ROW_TILE, wu.shape[2]), BF16)],
        compiler_params=_params("parallel"),
        name="proj_mlp",
    )(x, wo, h, vec(npost), vec(n1), wu, wd, vec(n2))


def _norm_mm_kernel(h_ref, nw_ref, w_ref, *o_refs, scale, split_maps):
    d = h_ref.shape[1]
    xn = _rms(h_ref[...], nw_ref[...]).astype(BF16)
    if split_maps:
        y = (jnp.dot(xn, w_ref[...], preferred_element_type=F32) * scale).astype(BF16)
        first = lax.broadcasted_iota(jnp.int32, y.shape, 1) % HEAD < HEAD // 2
        zero = jnp.zeros_like(y)
        o_refs[0][...] = jnp.where(first, y, zero)
        o_refs[1][...] = jnp.where(first, zero, y)
        return
    for c, o_ref in enumerate(o_refs):
        y = jnp.dot(xn, w_ref[:, c * d:(c + 1) * d], preferred_element_type=F32)
        o_ref[...] = (y * scale).astype(BF16)


def _norm_mm(h, nw, w, layer, scale=1.0, split_maps=False):
    t, d = h.shape
    n_out = 2 if split_maps else w.shape[2] // d
    row = lambda i: (i, 0)
    return pl.pallas_call(
        functools.partial(_norm_mm_kernel, scale=scale, split_maps=split_maps),
        grid=(t // ROW_TILE,),
        in_specs=[pl.BlockSpec((ROW_TILE, d), row), _resident((1, d)), _layer_spec(w, layer)],
        out_specs=[pl.BlockSpec((ROW_TILE, d), row)] * n_out,
        out_shape=[jax.ShapeDtypeStruct((t, d), BF16)] * n_out,
        compiler_params=_params("parallel"),
        name="norm_mm",
    )(h, nw.reshape(1, d).astype(F32), w)


def _kv_proj_kernel(h_ref, nw_ref, w_ref, k_ref, vt_ref, *, nheads):
    d = h_ref.shape[1]
    width = ATTN_GROUP * ATTN_TILE
    xn = _rms(h_ref[...], nw_ref[...]).astype(BF16)
    k_ref[...] = jnp.dot(xn, w_ref[:, 0:d], preferred_element_type=F32).astype(BF16)
    v = jnp.dot(xn, w_ref[:, d:2 * d], preferred_element_type=F32)
    for hd in range(nheads):
        v_t = v[:, hd * HEAD:(hd + 1) * HEAD].T.astype(BF16)
        for s in range(h_ref.shape[0] // width):
            vt_ref[hd, s] = v_t[:, s * width:(s + 1) * width]


def _kv_proj(h, nw, w, batch):
    t, d = h.shape
    seq = t // batch
    nheads = d // HEAD
    width = ATTN_GROUP * ATTN_TILE
    tiles_per_seq = seq // ROW_TILE
    steps = ROW_TILE // width
    row = lambda i: (i, 0)
    return pl.pallas_call(
        functools.partial(_kv_proj_kernel, nheads=nheads),
        grid=(t // ROW_TILE,),
        in_specs=[pl.BlockSpec((ROW_TILE, d), row), _resident((1, d)), _layer_spec(w, 0)],
        out_specs=[pl.BlockSpec((ROW_TILE, d), row),
                   pl.BlockSpec((nheads, steps, HEAD, width),
                                lambda i: (i // tiles_per_seq, i % tiles_per_seq, 0, 0))],
        out_shape=[jax.ShapeDtypeStruct((t, d), BF16),
                   jax.ShapeDtypeStruct((batch * nheads, seq // width, HEAD, width), BF16)],
        compiler_params=_params("parallel"),
        name="kv_proj",
    )(h, nw.reshape(1, d).astype(F32), w)


def _rel_bucket(rel):
    half = REL_BUCKETS // 2
    max_exact = half // 2
    ret = jnp.where(rel > 0, half, 0)
    n = jnp.abs(rel)
    nf = jnp.maximum(n, 1).astype(jnp.float32)
    large = max_exact + (jnp.log(nf / max_exact) / math.log(REL_MAX_DIST / max_exact)
                         * (half - max_exact)).astype(jnp.int32)
    large = jnp.minimum(large, half - 1)
    return ret + jnp.where(n < max_exact, n, large)


def _bias_tables(rel_bias, nheads):
    tile = ATTN_TILE
    kp = jnp.arange(tile)[:, None]
    qp = jnp.arange(tile)[None, :]
    table = rel_bias.astype(F32) * LOG2E

    def lookup(rel):
        onehot = (_rel_bucket(rel)[..., None] == jnp.arange(REL_BUCKETS)).astype(F32)
        b = jnp.dot(onehot, table, precision=lax.Precision.HIGHEST)
        return b.reshape(tile, tile, nheads, 2).transpose(2, 3, 0, 1)

    far = table[_rel_bucket(jnp.int32(-(tile + 1)))]
    kinds = [None] * 4
    kinds[BIAS_NEAR] = lookup(kp - tile - qp)
    kinds[BIAS_DIAG] = jnp.where(kp // CHUNK <= qp // CHUNK, lookup(kp - qp), NEG_INF)
    kinds[BIAS_FAR] = jnp.broadcast_to(far.reshape(nheads, 2, 1, 1), (nheads, 2, tile, tile))
    kinds[BIAS_MASKED] = jnp.full((nheads, 2, tile, tile), NEG_INF, F32)
    return jnp.stack(kinds, axis=2), far


def _attn_kernel(lam_ref, far_ref, qa_ref, qb_ref, k_ref, vt_ref, bias_ref, sw_ref, o_ref,
                 m_ref, l_ref, alpha_ref, acc_ref, s_ref, p_ref, *, out_scale):
    h = pl.program_id(1)
    qi = pl.program_id(2)
    i0 = qi * ATTN_QTILES
    group = ATTN_GROUP
    tile = ATTN_TILE
    width = group * tile
    qrows = ATTN_QTILES * tile
    nsteps = (i0 + ATTN_QTILES - 1) // group + 1

    def logits(t, q_index):
        start = pl.multiple_of(t * width, width)
        kblk = k_ref[pl.ds(start, width), :]
        q_start = pl.multiple_of(q_index * qrows, qrows)
        for m, q_ref in enumerate((qa_ref, qb_ref)):
            s_ref[m, :, 0:qrows] = lax.dot_general(kblk, q_ref[pl.ds(q_start, qrows), :], (((1,), (1,)), ((), ())),
                                                   preferred_element_type=F32)

    def tile_kind(j, i):
        return jnp.where(j <= i - 2, BIAS_FAR,
                         jnp.where(j == i - 1, BIAS_NEAR, jnp.where(j == i, BIAS_DIAG, BIAS_MASKED)))

    fold = lambda x, op: op(x.reshape(x.shape[0] // 8, 8, x.shape[1]), axis=0)
    pieces = range(width // ATTN_ROWS)

    def softmax(t, edge):
        for m in range(2):
            if edge:
                s = s_ref[m, :, 0:qrows]
                s = jnp.concatenate(
                    [jnp.concatenate(
                        [s[g * tile:(g + 1) * tile, u * tile:(u + 1) * tile]
                         + bias_ref[0, m, tile_kind(t * group + g, i0 + u)] for u in range(ATTN_QTILES)], axis=1)
                     for g in range(group)], axis=0)
                m_cur = jnp.max(s, axis=0, keepdims=True)
            else:
                far_c = far_ref[2 * h + m]
                col_max = None
                for c in pieces:
                    part = fold(s_ref[m, c * ATTN_ROWS:(c + 1) * ATTN_ROWS, 0:qrows], jnp.max)
                    col_max = part if col_max is None else jnp.maximum(col_max, part)
                m_cur = jnp.max(col_max, axis=0, keepdims=True) + far_c
            m_prev = m_ref[m]
            m_new = jnp.maximum(m_prev, m_cur)
            alpha = jnp.exp2(m_prev - m_new)
            if edge:
                p = jnp.exp2(s - m_new)
                p_ref[m, :, 0:qrows] = p.astype(BF16)
                p_sum = jnp.sum(p, axis=0, keepdims=True)
            else:
                col_sum = None
                for c in pieces:
                    rows = slice(c * ATTN_ROWS, (c + 1) * ATTN_ROWS)
                    p = jnp.exp2(s_ref[m, rows, 0:qrows] - (m_new - far_c))
                    p_ref[m, rows, 0:qrows] = p.astype(BF16)
                    part = fold(p, jnp.sum)
                    col_sum = part if col_sum is None else col_sum + part
                p_sum = jnp.sum(col_sum, axis=0, keepdims=True)
            l_ref[m] = alpha * l_ref[m] + p_sum
            alpha_ref[m] = alpha
            m_ref[m] = m_new

    def values(t):
        vtblk = vt_ref[0, jnp.maximum(t, 0)]
        for m in range(2):
            acc_ref[m, :, 0:qrows] = (alpha_ref[m] * acc_ref[m, :, 0:qrows]
                                      + jnp.dot(vtblk, p_ref[m, :, 0:qrows], preferred_element_type=F32))

    @pl.when(qi == 0)
    def _():
        l_ref[...] = jnp.zeros_like(l_ref)
        acc_ref[...] = jnp.zeros_like(acc_ref)
        p_ref[...] = jnp.zeros_like(p_ref)
        logits(0, qi)

    m_ref[...] = jnp.full_like(m_ref, NEG_INF)
    alpha_ref[...] = jnp.ones_like(alpha_ref)

    def far_step(t):
        values(t - 1)
        softmax(t, edge=False)
        logits(t + 1, qi)

    def far_group(u, carry):
        for j in range(ATTN_UNROLL):
            far_step(ATTN_UNROLL * u + j)
        return carry

    def far_single(t, carry):
        far_step(t)
        return carry

    nfar = jnp.maximum(nsteps - 2, 0)
    ngroups = nfar // ATTN_UNROLL
    lax.fori_loop(0, ngroups, far_group, 0)
    lax.fori_loop(ngroups * ATTN_UNROLL, nfar, far_single, 0)

    @pl.when(nsteps >= 2)
    def _():
        values(nsteps - 3)
        softmax(nsteps - 2, edge=True)
        logits(nsteps - 1, qi)

    values(nsteps - 2)
    softmax(nsteps - 1, edge=True)
    values(nsteps - 1)
    logits(0, jnp.minimum(qi + 1, pl.num_programs(2) - 1))

    o_t = acc_ref[0, :, 0:qrows] / l_ref[0] - lam_ref[0] * (acc_ref[1, :, 0:qrows] / l_ref[1])
    o_ref[...] = (_rms(o_t.T, sw_ref[...]) * out_scale).astype(BF16)


def _diff_attn(qa, qb, k, vt, bias_tiles, far, lam, subln_w, batch, out_scale):
    t, d = k.shape
    seq = t // batch
    nheads = d // HEAD
    tile = ATTN_TILE
    width = ATTN_GROUP * tile
    qrows = ATTN_QTILES * tile
    nq = seq // qrows
    smem = pl.BlockSpec(memory_space=pltpu.SMEM)
    head_spec = pl.BlockSpec((seq, HEAD), lambda b, h, i: (b, h))
    return pl.pallas_call(
        functools.partial(_attn_kernel, out_scale=out_scale),
        grid=(batch, nheads, nq),
        in_specs=[smem, smem, head_spec, head_spec, head_spec,
                  pl.BlockSpec((1, seq // width, HEAD, width), lambda b, h, i: (b * nheads + h, 0, 0, 0)),
                  pl.BlockSpec((1, 2, 4, tile, tile), lambda b, h, i: (h, 0, 0, 0, 0)),
                  pl.BlockSpec((1, HEAD), lambda b, h, i: (0, 0))],
        out_specs=pl.BlockSpec((qrows, HEAD), lambda b, h, i: (b * nq + i, h)),
        out_shape=jax.ShapeDtypeStruct((t, d), BF16),
        scratch_shapes=[pltpu.VMEM((2, 1, qrows), F32)] * 3
        + [pltpu.VMEM((2, HEAD, qrows + ATTN_PAD), F32), pltpu.VMEM((2, width, qrows + ATTN_PAD), F32),
           pltpu.VMEM((2, width, qrows + ATTN_PAD), BF16)],
        compiler_params=_params("parallel", "parallel", "arbitrary"),
        name="diff_attn",
    )(lam.reshape(1).astype(F32), far, qa, qb, k, vt, bias_tiles, subln_w.reshape(1, HEAD).astype(F32))


def kernel(x, a_norm_pre, a_norm_post, a_w_in, a_lb, a_gate_norm, a_w_out, kv_norm, w_kv,
           b_norm_pre, b_norm_post, b_w_q, b_lambda, b_subln, b_w_out, rel_bias,
           mlp_norm_pre, mlp_norm_post, mlp_w_up, mlp_w_down):
    batch, seq, d = x.shape
    n_a = a_w_in.shape[0]
    n_b = b_w_q.shape[0]
    nheads = d // HEAD
    w_in, w_a_out = a_w_in.astype(BF16), a_w_out.astype(BF16)
    w_q, w_b_out = b_w_q.astype(BF16), b_w_out.astype(BF16)
    w_up, w_down = mlp_w_up.astype(BF16), mlp_w_down.astype(BF16)

    lb_all = jnp.cumsum(jax.nn.softmax(a_lb.astype(F32), axis=0), axis=0)
    lb_all = lb_all - lb_all[0:1]

    h = x.reshape(batch * seq, d)
    for a in range(n_a):
        q, b, k, v, sg = _hgrn_in(h, a_norm_pre[a], w_in, a, lb_all[a])
        o = _hgrn_rec(q, b, k, v, sg, a_gate_norm[a], batch)
        h = _proj_mlp(o, w_a_out, a, h, a_norm_post[a],
                      mlp_norm_pre[a], w_up, w_down, a, mlp_norm_post[a])

    k_sh, vt = _kv_proj(h, kv_norm, w_kv.astype(BF16)[None], batch)
    bias_tiles, far = _bias_tables(rel_bias, nheads)
    q_scale = (HEAD // 2) ** -0.5 * LOG2E
    for bi in range(n_b):
        layer = n_a + bi
        lam_init = 0.8 - 0.6 * math.exp(-0.3 * layer)
        lp = b_lambda[bi].astype(F32)
        lam = jnp.exp(jnp.sum(lp[0] * lp[1])) - jnp.exp(jnp.sum(lp[2] * lp[3])) + lam_init
        qa, qb = _norm_mm(h, b_norm_pre[bi], w_q, bi, scale=q_scale, split_maps=True)
        o = _diff_attn(qa, qb, k_sh, vt, bias_tiles, far, lam, b_subln[bi], batch, 1.0 - lam_init)
        h = _proj_mlp(o, w_b_out, bi, h, b_norm_post[bi],
                      mlp_norm_pre[layer], w_up, w_down, layer, mlp_norm_post[layer])
    return h.reshape(batch, seq, d)
```
